```python
import math
import jax, jax.numpy as jnp
from jax import lax
import numpy as np

D_MODEL = 2048
BATCH = 2
SEQ = 4096
DEPTH = 4
DEC_BATCH = 8
DEC_SEQ = 8
PAST_LEN = 16384
PAGE_SIZE = 128

ATTN_HEAD_DIM = 128
ATTN_WIDTH = D_MODEL // 2
ATTN_HEADS = ATTN_WIDTH // ATTN_HEAD_DIM
SSM_WIDTH = D_MODEL - ATTN_WIDTH
SSM_GROUP = 16
SSM_GROUPS = SSM_WIDTH // SSM_GROUP
SSM_STATE = 64
IN_COLS = 3 * ATTN_WIDTH + SSM_WIDTH
D_FF = ((8 * D_MODEL // 3 + 255) // 256) * 256
Q_BLOCK = 128
RMS_EPS = 1e-6
DT_MIN = 1e-3
DT_MAX = 1e-1
SB_BIAS_INIT = -6.0

kernel_name = 'hymba_stickbreak_s5_macaron_step'


def rms_norm(x, g):
    xf = x.astype(jnp.float32)
    y = xf * lax.rsqrt(jnp.mean(xf * xf, axis=-1, keepdims=True) + RMS_EPS)
    return (y * g.astype(jnp.float32)).astype(x.dtype)


def ffn_half(h, g_pre, w_gate, w_up, w_down, g_post):
    n = rms_norm(h, g_pre)
    f = (jax.nn.silu(n @ w_gate) * (n @ w_up)) @ w_down
    return h + 0.5 * rms_norm(f, g_post)


def _sb_block(qb, qpos, kf, vf, kpos, bias):
    z = jnp.einsum('bqhd,bkhd->bhqk', qb.astype(jnp.float32), kf) * (ATTN_HEAD_DIM ** -0.5)
    z = z + bias[None, :, None, None]
    mask = (kpos[None, :] < qpos[:, None])[None, None]
    log_keep = jnp.where(mask, jax.nn.log_sigmoid(-z), 0.0)
    rc = lax.cumsum(log_keep, axis=3, reverse=True)
    after = jnp.concatenate([rc[..., 1:], jnp.zeros_like(rc[..., :1])], axis=-1)
    w = jnp.where(mask, jnp.exp(jax.nn.log_sigmoid(z) + after), 0.0)
    return jnp.einsum('bhqk,bkhd->bqhd', w, vf)


def stick_breaking_attention(q, k, v, q_pos, k_pos, bias):
    bsz, tq, nh, hd = q.shape
    kf = k.astype(jnp.float32)
    vf = v.astype(jnp.float32)
    bf = bias.astype(jnp.float32)
    if tq <= Q_BLOCK:
        out = _sb_block(q, q_pos, kf, vf, k_pos, bf)
    else:
        nb = tq // Q_BLOCK
        qs = q.reshape(bsz, nb, Q_BLOCK, nh, hd).transpose(1, 0, 2, 3, 4)
        ps = q_pos.reshape(nb, Q_BLOCK)
        out = lax.map(lambda a: _sb_block(a[0], a[1], kf, vf, k_pos, bf), (qs, ps))
        out = out.transpose(1, 0, 2, 3, 4).reshape(bsz, tq, nh, hd)
    return out.astype(q.dtype)


def _linear_combine(e1, e2):
    a1, b1 = e1
    a2, b2 = e2
    return a1 * a2, a2 * b1 + b2


def s5_scan(u, h0, lam_re, lam_im, log_dt, b_re, b_im, c_re, c_im, d_skip):
    f32 = jnp.float32
    lam = lax.complex(lam_re.astype(f32), lam_im.astype(f32))
    dt = jnp.exp(log_dt.astype(f32))[:, None]
    lam_bar = jnp.exp(lam * dt)
    b_bar = ((lam_bar - 1.0) / lam)[..., None] * lax.complex(b_re.astype(f32), b_im.astype(f32))
    uf = u.astype(f32)
    bu = jnp.einsum('gpc,btgc->btgp', b_bar, uf.astype(jnp.complex64))
    if h0 is not None:
        bu = bu.at[:, 0].add(lam_bar * h0)
    a = jnp.broadcast_to(lam_bar, bu.shape)
    _, hs = lax.associative_scan(_linear_combine, (a, bu), axis=1)
    c = lax.complex(c_re.astype(f32), c_im.astype(f32))
    y = jnp.einsum('gcp,btgp->btgc', c, hs).real + d_skip.astype(f32) * uf
    return y.reshape(u.shape[0], u.shape[1], -1).astype(u.dtype), hs[:, -1]


def token_mixing(h, k_past, v_past, h0, q_pos, k_pos, g_pre, w_in, sb_bias, lam_re, lam_im, log_dt,
                 b_re, b_im, c_re, c_im, d_skip, w_glu, g_attn, g_ssm, w_out, g_post):
    bsz, t, _ = h.shape
    z = rms_norm(h, g_pre) @ w_in
    q, k, v, u = jnp.split(z, [ATTN_WIDTH, 2 * ATTN_WIDTH, 3 * ATTN_WIDTH], axis=-1)
    q = q.reshape(bsz, t, ATTN_HEADS, ATTN_HEAD_DIM)
    k = k.reshape(bsz, t, ATTN_HEADS, ATTN_HEAD_DIM)
    v = v.reshape(bsz, t, ATTN_HEADS, ATTN_HEAD_DIM)
    k_ctx = k if k_past is None else jnp.concatenate([k_past.astype(k.dtype), k], axis=1)
    v_ctx = v if v_past is None else jnp.concatenate([v_past.astype(v.dtype), v], axis=1)
    attn = stick_breaking_attention(q, k_ctx, v_ctx, q_pos, k_pos, sb_bias).reshape(bsz, t, ATTN_WIDTH)
    y, h_last = s5_scan(u.reshape(bsz, t, SSM_GROUPS, SSM_GROUP), h0, lam_re, lam_im, log_dt,
                        b_re, b_im, c_re, c_im, d_skip)
    g = jax.nn.gelu(y)
    ssm_out = g * jax.nn.sigmoid(g @ w_glu)
    merged = jnp.concatenate([rms_norm(attn, g_attn), rms_norm(ssm_out, g_ssm)], axis=-1)
    out = rms_norm(merged @ w_out, g_post)
    return h + out, k, v, h_last


def setup_inputs(seed: int = 0) -> dict:
    key = jax.random.key(seed)
    ks = iter(jax.random.split(key, 48))
    f32 = jnp.float32

    def nrm(shape, scale):
        return jax.random.normal(next(ks), shape, f32) * scale

    def gain(shape):
        return 1.0 + 0.02 * jax.random.normal(next(ks), shape, f32)

    n_pages = PAST_LEN // PAGE_SIZE
    n_used = DEC_BATCH * n_pages
    n_pool = n_used + max(1, n_used // 4)
    x_prompt = nrm((BATCH, SEQ, D_MODEL), 1.0)
    x_sample = nrm((DEC_BATCH, DEC_SEQ, D_MODEL), 1.0)
    cache_k = nrm((DEPTH, n_pool, PAGE_SIZE, ATTN_HEADS, ATTN_HEAD_DIM), 1.0)
    cache_v = nrm((DEPTH, n_pool, PAGE_SIZE, ATTN_HEADS, ATTN_HEAD_DIM), 1.0)
    state_ssm_re = nrm((DEPTH, DEC_BATCH, SSM_GROUPS, SSM_STATE), 0.5)
    state_ssm_im = nrm((DEPTH, DEC_BATCH, SSM_GROUPS, SSM_STATE), 0.5)
    page_table = jax.random.permutation(next(ks), n_pool)[:n_used].reshape(DEC_BATCH, n_pages).astype(jnp.int32)

    d = DEPTH
    inputs = {
        'x_prompt': x_prompt,
        'x_sample': x_sample,
        'cache_k': cache_k,
        'cache_v': cache_v,
        'state_ssm_re': state_ssm_re,
        'state_ssm_im': state_ssm_im,
        'page_table': page_table,
        'g_ffn1_pre': gain((d, D_MODEL)),
        'w_ffn1_gate': nrm((d, D_MODEL, D_FF), D_MODEL ** -0.5),
        'w_ffn1_up': nrm((d, D_MODEL, D_FF), D_MODEL ** -0.5),
        'w_ffn1_down': nrm((d, D_FF, D_MODEL), D_FF ** -0.5),
        'g_ffn1_post': gain((d, D_MODEL)),
        'g_mix_pre': gain((d, D_MODEL)),
        'w_in': nrm((d, D_MODEL, IN_COLS), D_MODEL ** -0.5),
        'sb_bias': SB_BIAS_INIT + nrm((d, ATTN_HEADS), 0.1),
        'lambda_re': -0.5 + nrm((d, SSM_GROUPS, SSM_STATE), 0.01),
        'lambda_im': math.pi * jnp.arange(SSM_STATE, dtype=f32)[None, None, :] + nrm((d, SSM_GROUPS, SSM_STATE), 0.01),
        'log_dt': jax.random.uniform(next(ks), (d, SSM_GROUPS), f32, math.log(DT_MIN), math.log(DT_MAX)),
        'b_re': nrm((d, SSM_GROUPS, SSM_STATE, SSM_GROUP), (2 * SSM_GROUP) ** -0.5),
        'b_im': nrm((d, SSM_GROUPS, SSM_STATE, SSM_GROUP), (2 * SSM_GROUP) ** -0.5),
        'c_re': nrm((d, SSM_GROUPS, SSM_GROUP, SSM_STATE), (2 * SSM_STATE) ** -0.5),
        'c_im': nrm((d, SSM_GROUPS, SSM_GROUP, SSM_STATE), (2 * SSM_STATE) ** -0.5),
        'd_skip': 1.0 + nrm((d, SSM_GROUPS, SSM_GROUP), 0.1),
        'w_glu': nrm((d, SSM_WIDTH, SSM_WIDTH), SSM_WIDTH ** -0.5),
        'g_attn_out': gain((d, ATTN_WIDTH)),
        'g_ssm_out': gain((d, SSM_WIDTH)),
        'w_out': nrm((d, D_MODEL, D_MODEL), D_MODEL ** -0.5),
        'g_mix_post': gain((d, D_MODEL)),
        'g_ffn2_pre': gain((d, D_MODEL)),
        'w_ffn2_gate': nrm((d, D_MODEL, D_FF), D_MODEL ** -0.5),
        'w_ffn2_up': nrm((d, D_MODEL, D_FF), D_MODEL ** -0.5),
        'w_ffn2_down': nrm((d, D_FF, D_MODEL), D_FF ** -0.5),
        'g_ffn2_post': gain((d, D_MODEL)),
    }
    return inputs


def reference(x_prompt, x_sample, cache_k, cache_v, state_ssm_re, state_ssm_im, page_table,
              g_ffn1_pre, w_ffn1_gate, w_ffn1_up, w_ffn1_down, g_ffn1_post,
              g_mix_pre, w_in, sb_bias, lambda_re, lambda_im, log_dt, b_re, b_im, c_re, c_im, d_skip,
              w_glu, g_attn_out, g_ssm_out, w_out, g_mix_post,
              g_ffn2_pre, w_ffn2_gate, w_ffn2_up, w_ffn2_down, g_ffn2_post):
    f32 = jnp.float32
    dec_b, n_pages = page_table.shape
    past = n_pages * cache_k.shape[2]
    t_p = x_prompt.shape[1]
    t_s = x_sample.shape[1]
    pos_p = jnp.arange(t_p, dtype=jnp.int32)
    qpos_s = past + jnp.arange(t_s, dtype=jnp.int32)
    kpos_s = jnp.arange(past + t_s, dtype=jnp.int32)

    hp, hs = x_prompt, x_sample
    k_p_rows, v_p_rows, k_s_rows, v_s_rows = [], [], [], []
    sp_re, sp_im, ss_re, ss_im = [], [], [], []
    for l in range(DEPTH):
        ffn1 = (g_ffn1_pre[l], w_ffn1_gate[l], w_ffn1_up[l], w_ffn1_down[l], g_ffn1_post[l])
        ffn2 = (g_ffn2_pre[l], w_ffn2_gate[l], w_ffn2_up[l], w_ffn2_down[l], g_ffn2_post[l])
        mix = (g_mix_pre[l], w_in[l], sb_bias[l], lambda_re[l], lambda_im[l], log_dt[l], b_re[l], b_im[l],
               c_re[l], c_im[l], d_skip[l], w_glu[l], g_attn_out[l], g_ssm_out[l], w_out[l], g_mix_post[l])

        hp = ffn_half(hp, *ffn1)
        hs = ffn_half(hs, *ffn1)

        hp, kp, vp, hlast_p = token_mixing(hp, None, None, None, pos_p, pos_p, *mix)

        k_past = cache_k[l][page_table].reshape(dec_b, past, ATTN_HEADS, ATTN_HEAD_DIM)
        v_past = cache_v[l][page_table].reshape(dec_b, past, ATTN_HEADS, ATTN_HEAD_DIM)
        h0 = lax.complex(state_ssm_re[l].astype(f32), state_ssm_im[l].astype(f32))
        hs, ks_new, vs_new, hlast_s = token_mixing(hs, k_past, v_past, h0, qpos_s, kpos_s, *mix)

        hp = ffn_half(hp, *ffn2)
        hs = ffn_half(hs, *ffn2)

        k_p_rows.append(kp)
        v_p_rows.append(vp)
        k_s_rows.append(ks_new)
        v_s_rows.append(vs_new)
        sp_re.append(hlast_p.real)
        sp_im.append(hlast_p.imag)
        ss_re.append(hlast_s.real)
        ss_im.append(hlast_s.imag)

    return (hp, hs,
            jnp.stack(k_p_rows), jnp.stack(v_p_rows),
            jnp.stack(k_s_rows), jnp.stack(v_s_rows),
            jnp.stack(sp_re), jnp.stack(sp_im),
            jnp.stack(ss_re), jnp.stack(ss_im))
```

```python
import functools
import math

import jax
import jax.numpy as jnp
from jax import lax
from jax.experimental import pallas as pl
from jax.experimental.pallas import tpu as pltpu

F32 = jnp.float32
BF16 = jnp.bfloat16

RMS_EPS = 1e-6
HEAD_DIM = 128
SSM_GROUP = 16
SSM_STATE = 64
GROUPS_PER_BLOCK = 8
SUBLANES = 8
VMEM_LIMIT = 56 * 1024 * 1024


def _params(*sem):
    return pltpu.CompilerParams(dimension_semantics=sem, vmem_limit_bytes=VMEM_LIMIT)


def _rms(x, g):
    return x * lax.rsqrt(jnp.mean(x * x, axis=-1, keepdims=True) + RMS_EPS) * g


def _softplus(z):
    return jnp.maximum(z, 0.0) + jnp.log1p(jnp.exp(-jnp.abs(z)))


def _nt_dot(a, b):
    return lax.dot_general(a, b, (((1,), (1,)), ((), ())), preferred_element_type=F32)


def _suffix_sums(lk, upper):
    hi = lk.astype(BF16)
    lo = (lk - hi.astype(F32)).astype(BF16)
    return (jnp.dot(hi, upper, preferred_element_type=F32)
            + jnp.dot(lo, upper, preferred_element_type=F32))


def _strict_upper(n):
    r = lax.broadcasted_iota(jnp.int32, (n, n), 0)
    c = lax.broadcasted_iota(jnp.int32, (n, n), 1)
    return jnp.where(r > c, 1.0, 0.0).astype(BF16)


def _ffn_kernel(x_ref, gpre_ref, wg_ref, wu_ref, wd_ref, gpost_ref, o_ref, n_ref, acc_ref):
    f = pl.program_id(1)

    @pl.when(f == 0)
    def _():
        n_ref[...] = _rms(x_ref[...], gpre_ref[...]).astype(BF16)
        acc_ref[...] = jnp.zeros_like(acc_ref)

    n = n_ref[...]
    gate = jnp.dot(n, wg_ref[...], preferred_element_type=F32)
    up = jnp.dot(n, wu_ref[...], preferred_element_type=F32)
    act = gate * jax.nn.sigmoid(gate) * up
    acc_ref[...] += jnp.dot(act.astype(BF16), wd_ref[...], preferred_element_type=F32)

    @pl.when(f == pl.num_programs(1) - 1)
    def _():
        o_ref[...] = x_ref[...] + 0.5 * _rms(acc_ref[...], gpost_ref[...])


def _ffn(x, g_pre, w_gate, w_up, w_down, g_post, *, tm, tf):
    m, d = x.shape
    dff = w_gate.shape[1]
    tm = min(tm, m)
    return pl.pallas_call(
        _ffn_kernel,
        grid=(m // tm, dff // tf),
        in_specs=[
            pl.BlockSpec((tm, d), lambda i, f: (i, 0)),
            pl.BlockSpec((1, d), lambda i, f: (0, 0)),
            pl.BlockSpec((d, tf), lambda i, f: (0, f)),
            pl.BlockSpec((d, tf), lambda i, f: (0, f)),
            pl.BlockSpec((tf, d), lambda i, f: (f, 0)),
            pl.BlockSpec((1, d), lambda i, f: (0, 0)),
        ],
        out_specs=pl.BlockSpec((tm, d), lambda i, f: (i, 0)),
        out_shape=jax.ShapeDtypeStruct((m, d), F32),
        scratch_shapes=[pltpu.VMEM((tm, d), BF16), pltpu.VMEM((tm, d), F32)],
        compiler_params=_params("parallel", "arbitrary"),
        name="ffn_half",
    )(x, g_pre, w_gate, w_up, w_down, g_post)


def _inproj_kernel(x_ref, g_ref, w_ref, o_ref, n_ref):
    @pl.when(pl.program_id(1) == 0)
    def _():
        n_ref[...] = _rms(x_ref[...], g_ref[...]).astype(BF16)

    o_ref[...] = jnp.dot(n_ref[...], w_ref[...], preferred_element_type=F32)


def _inproj(x, g, w_in, *, tm):
    m, d = x.shape
    cols = w_in.shape[1]
    width = cols // 4
    tm = min(tm, m)
    return pl.pallas_call(
        _inproj_kernel,
        grid=(m // tm, 4),
        in_specs=[
            pl.BlockSpec((tm, d), lambda i, j: (i, 0)),
            pl.BlockSpec((1, d), lambda i, j: (0, 0)),
            pl.BlockSpec((d, width), lambda i, j: (0, j)),
        ],
        out_specs=pl.BlockSpec((None, tm, width), lambda i, j: (j, i, 0)),
        out_shape=jax.ShapeDtypeStruct((4, m, width), F32),
        scratch_shapes=[pltpu.VMEM((tm, d), BF16)],
        compiler_params=_params("parallel", "arbitrary"),
        name="in_proj",
    )(x, g, w_in)


def _attn_kernel(bias_ref, q_ref, k_ref, v_ref, o_ref, *, tq, scale):
    h = pl.program_id(1)
    i = pl.program_id(2)
    bias = bias_ref[h]
    q = q_ref[...].astype(BF16)
    upper = _strict_upper(tq)
    row = lax.broadcasted_iota(jnp.int32, (tq, tq), 0)
    col = lax.broadcasted_iota(jnp.int32, (tq, tq), 1)
    causal = col < row

    def block(j, carry, acc, masked):
        start = pl.multiple_of(j * tq, tq)
        kb = k_ref[pl.ds(start, tq), :].astype(BF16)
        vb = v_ref[pl.ds(start, tq), :].astype(BF16)
        z = _nt_dot(q, kb) * scale + bias
        sp = _softplus(z)
        lk = -sp
        if masked:
            lk = jnp.where(causal, lk, 0.0)
        after = _suffix_sums(lk, upper)
        w = jnp.exp((z - sp) + after + carry)
        if masked:
            w = jnp.where(causal, w, 0.0)
        acc = acc + jnp.dot(w.astype(BF16), vb, preferred_element_type=F32)
        carry = carry + (after[:, :1] + lk[:, :1])
        return carry, acc

    carry0 = jnp.zeros((tq, 1), F32)
    acc0 = jnp.zeros((tq, HEAD_DIM), F32)
    carry, acc = block(i, carry0, acc0, True)

    def body(t, c):
        return block(i - 1 - t, c[0], c[1], False)

    carry, acc = lax.fori_loop(0, i, body, (carry, acc))
    o_ref[...] = acc


def _attn_prompt(z4, sb_bias, *, tq):
    _, b, t, width = z4.shape
    heads = width // HEAD_DIM
    tq = min(tq, t)
    kern = functools.partial(_attn_kernel, tq=tq, scale=HEAD_DIM ** -0.5)
    return pl.pallas_call(
        kern,
        grid=(b, heads, t // tq),
        in_specs=[
            pl.BlockSpec(memory_space=pltpu.SMEM),
            pl.BlockSpec((None, None, tq, HEAD_DIM), lambda bi, h, i: (0, bi, i, h)),
            pl.BlockSpec((None, None, t, HEAD_DIM), lambda bi, h, i: (1, bi, 0, h)),
            pl.BlockSpec((None, None, t, HEAD_DIM), lambda bi, h, i: (2, bi, 0, h)),
        ],
        out_specs=pl.BlockSpec((None, tq, HEAD_DIM), lambda bi, h, i: (bi, i, h)),
        out_shape=jax.ShapeDtypeStruct((b, t, width), F32),
        compiler_params=_params("parallel", "parallel", "arbitrary"),
        name="attn_prompt",
    )(sb_bias, z4, z4, z4)


def _sattn_kernel(pt_ref, qbd_ref, bias_ref, kn_ref, vn_ref, kc_ref, vc_ref, o_ref,
                  acc_ref, carry_ref, *, scale, heads, tsteps):
    j = pl.program_id(1)
    rows, page = carry_ref.shape
    upper = _strict_upper(page)
    qbd = qbd_ref[...]
    bias = bias_ref[...]

    def process(k_ref, v_ref, masked):
        kb = k_ref[...].astype(BF16)
        vb = v_ref[...].astype(BF16)
        z = _nt_dot(qbd, kb) * scale + bias
        sp = _softplus(z)
        lk = -sp
        if masked:
            row = lax.broadcasted_iota(jnp.int32, (rows, page), 0)
            col = lax.broadcasted_iota(jnp.int32, (rows, page), 1)
            causal = col < (row % tsteps)
            lk = jnp.where(causal, lk, 0.0)
        after = _suffix_sums(lk, upper)
        w = jnp.exp((z - sp) + after + carry_ref[...])
        if masked:
            w = jnp.where(causal, w, 0.0)
        acc_ref[...] += jnp.dot(w.astype(BF16), vb, preferred_element_type=F32)
        carry_ref[...] += jnp.broadcast_to(after[:, :1] + lk[:, :1], (rows, page))

    @pl.when(j == 0)
    def _():
        acc_ref[...] = jnp.zeros_like(acc_ref)
        carry_ref[...] = jnp.zeros_like(carry_ref)
        process(kn_ref, vn_ref, True)

    process(kc_ref, vc_ref, False)

    @pl.when(j == pl.num_programs(1) - 1)
    def _():
        for h in range(heads):
            o_ref[:, h * HEAD_DIM:(h + 1) * HEAD_DIM] = (
                acc_ref[h * tsteps:(h + 1) * tsteps, h * HEAD_DIM:(h + 1) * HEAD_DIM])


def _attn_sample(zs4, cache_k, cache_v, layer, page_table, sb_bias):
    _, b, ts, width = zs4.shape
    heads = width // HEAD_DIM
    page = cache_k.shape[2]
    n_pages = page_table.shape[1]
    rows = heads * ts
    q = zs4[0].reshape(b, ts, heads, HEAD_DIM)
    eye = jnp.eye(heads, dtype=F32)
    qbd = (q.transpose(0, 2, 1, 3)[:, :, :, None, :] * eye[None, :, None, :, None])
    qbd = qbd.reshape(b, rows, width).astype(BF16)
    bias = jnp.broadcast_to(jnp.repeat(sb_bias, ts)[:, None], (rows, page)).astype(F32)
    pad = ((0, 0), (0, page - ts), (0, 0))
    k_new = jnp.pad(zs4[1], pad)
    v_new = jnp.pad(zs4[2], pad)
    kern = functools.partial(_sattn_kernel, scale=HEAD_DIM ** -0.5, heads=heads, tsteps=ts)
    last = n_pages - 1
    return pl.pallas_call(
        kern,
        grid_spec=pltpu.PrefetchScalarGridSpec(
            num_scalar_prefetch=1,
            grid=(b, n_pages),
            in_specs=[
                pl.BlockSpec((None, rows, width), lambda bi, j, pt: (bi, 0, 0)),
                pl.BlockSpec((rows, page), lambda bi, j, pt: (0, 0)),
                pl.BlockSpec((None, page, width), lambda bi, j, pt: (bi, 0, 0)),
                pl.BlockSpec((None, page, width), lambda bi, j, pt: (bi, 0, 0)),
                pl.BlockSpec((None, None, page, width),
                             lambda bi, j, pt: (layer, pt[bi, last - j], 0, 0)),
                pl.BlockSpec((None, None, page, width),
                             lambda bi, j, pt: (layer, pt[bi, last - j], 0, 0)),
            ],
            out_specs=pl.BlockSpec((None, ts, width), lambda bi, j, pt: (bi, 0, 0)),
            scratch_shapes=[pltpu.VMEM((rows, width), F32), pltpu.VMEM((rows, page), F32)],
        ),
        out_shape=jax.ShapeDtypeStruct((b, ts, width), F32),
        compiler_params=_params("parallel", "arbitrary"),
        name="attn_sample",
    )(page_table, qbd, bias, k_new, v_new, cache_k, cache_v)


def _ssm_kernel(u_ref, h0_ref, bbd_ref, cbd_ref, d_ref, tbl_ref, y_ref, hl_ref, s_ref, carry_ref):
    t = pl.program_id(2)
    tc, lanes = s_ref.shape
    half = lanes // 2

    @pl.when(t == 0)
    def _():
        carry_ref[...] = jnp.broadcast_to(h0_ref[...], carry_ref.shape)

    u = u_ref[...]
    s_ref[...] = jnp.dot(u.astype(BF16), bbd_ref[...], preferred_element_type=F32)

    def tile(k, c):
        r0 = pl.multiple_of(k * SUBLANES, SUBLANES)
        xr = s_ref[pl.ds(r0, SUBLANES), :half]
        xi = s_ref[pl.ds(r0, SUBLANES), half:]
        for n, d in enumerate((1, 2, 4)):
            lr = tbl_ref[2 * n]
            li = tbl_ref[2 * n + 1]
            sr = pltpu.roll(xr, d, 0)
            si = pltpu.roll(xi, d, 0)
            xr, xi = xr + (lr * sr - li * si), xi + (lr * si + li * sr)
        pr = tbl_ref[6]
        pi = tbl_ref[7]
        cr, ci = c
        hr = xr + (pr * cr - pi * ci)
        hi = xi + (pr * ci + pi * cr)
        s_ref[pl.ds(r0, SUBLANES), :half] = hr
        s_ref[pl.ds(r0, SUBLANES), half:] = hi
        last = SUBLANES - 1
        return (jnp.broadcast_to(hr[last:, :], hr.shape), jnp.broadcast_to(hi[last:, :], hi.shape))

    cr, ci = lax.fori_loop(0, tc // SUBLANES, tile, (carry_ref[:, :half], carry_ref[:, half:]))
    carry_ref[:, :half] = cr
    carry_ref[:, half:] = ci
    y_ref[...] = (jnp.dot(s_ref[...].astype(BF16), cbd_ref[...], preferred_element_type=F32)
                  + d_ref[...] * u)

    @pl.when(t == pl.num_programs(2) - 1)
    def _():
        hl_ref[...] = carry_ref[:1, :]


def _ssm_tables(lam_re, lam_im, log_dt, b_re, b_im, c_re, c_im, d_skip):
    g, p = lam_re.shape
    c = b_re.shape[-1]
    nb = g // GROUPS_PER_BLOCK
    gb = GROUPS_PER_BLOCK
    lam = lax.complex(lam_re, lam_im)
    dt = jnp.exp(log_dt)[:, None]
    lam_bar = jnp.exp(lam * dt)
    b_bar = ((lam_bar - 1.0) / lam)[..., None] * lax.complex(b_re, b_im)
    eye = jnp.eye(gb, dtype=F32)

    def blockdiag_in(x):
        x = x.reshape(nb, gb, p, c).transpose(0, 1, 3, 2)
        x = x[:, :, :, None, :] * eye[None, :, None, :, None]
        return x.reshape(nb, gb * c, gb * p)

    bbd = jnp.concatenate([blockdiag_in(b_bar.real), blockdiag_in(b_bar.imag)], axis=-1)

    def blockdiag_out(x):
        x = x.reshape(nb, gb, c, p).transpose(0, 1, 3, 2)
        x = x[:, :, :, None, :] * eye[None, :, None, :, None]
        return x.reshape(nb, gb * p, gb * c)

    cbd = jnp.concatenate([blockdiag_out(c_re), -blockdiag_out(c_im)], axis=1)

    pw = [lam_bar]
    for _ in range(SUBLANES - 1):
        pw.append(pw[-1] * lam_bar)
    pw = jnp.stack(pw)
    rows = jnp.arange(SUBLANES)[:, None, None]
    tbls = []
    for d in (1, 2, 4):
        m = jnp.where(rows >= d, jnp.broadcast_to(pw[d - 1], pw.shape), 0.0)
        tbls += [m.real, m.imag]
    tbls += [pw.real, pw.imag]
    tbl = jnp.stack(tbls)
    tbl = tbl.reshape(8, SUBLANES, nb, gb * p).transpose(2, 0, 1, 3)
    dsk = d_skip.reshape(nb, 1, gb * c)
    return bbd.astype(BF16), cbd.astype(BF16), dsk, tbl


def _ssm(z4, h0, tables, *, tc):
    bbd, cbd, dsk, tbl = tables
    _, b, t, width = z4.shape
    nb, uw, sw = bbd.shape
    tc = min(tc, t)
    return pl.pallas_call(
        _ssm_kernel,
        grid=(b, nb, t // tc),
        in_specs=[
            pl.BlockSpec((None, None, tc, uw), lambda bi, g, ti: (3, bi, ti, g)),
            pl.BlockSpec((None, None, 1, sw), lambda bi, g, ti: (bi, g, 0, 0)),
            pl.BlockSpec((None, uw, sw), lambda bi, g, ti: (g, 0, 0)),
            pl.BlockSpec((None, sw, uw), lambda bi, g, ti: (g, 0, 0)),
            pl.BlockSpec((None, 1, uw), lambda bi, g, ti: (g, 0, 0)),
            pl.BlockSpec((None, 8, SUBLANES, sw // 2), lambda bi, g, ti: (g, 0, 0, 0)),
        ],
        out_specs=[
            pl.BlockSpec((None, tc, uw), lambda bi, g, ti: (bi, ti, g)),
            pl.BlockSpec((None, None, 1, sw), lambda bi, g, ti: (bi, g, 0, 0)),
        ],
        out_shape=[jax.ShapeDtypeStruct((b, t, width), F32),
                   jax.ShapeDtypeStruct((b, nb, 1, sw), F32)],
        scratch_shapes=[pltpu.VMEM((tc, sw), F32), pltpu.VMEM((SUBLANES, sw), F32)],
        compiler_params=_params("parallel", "parallel", "arbitrary"),
        name="s5_scan",
    )(z4, h0, bbd, cbd, dsk, tbl)


def _pack_state(re, im):
    b, g, p = re.shape
    nb = g // GROUPS_PER_BLOCK
    re = re.reshape(b, nb, 1, GROUPS_PER_BLOCK * p)
    im = im.reshape(b, nb, 1, GROUPS_PER_BLOCK * p)
    return jnp.concatenate([re, im], axis=-1)


def _unpack_state(hl, g, p):
    b = hl.shape[0]
    half = hl.shape[-1] // 2
    return hl[..., :half].reshape(b, g, p), hl[..., half:].reshape(b, g, p)


def _mix_kernel(h_ref, a_ref, y_ref, wglu_ref, gattn_ref, gssm_ref, wout_ref, gpost_ref, o_ref):
    y = y_ref[...]
    g = 0.5 * y * (1.0 + jnp.tanh(math.sqrt(2.0 / math.pi) * (y + 0.044715 * (y * y * y))))
    gate = jnp.dot(g.astype(BF16), wglu_ref[...], preferred_element_type=F32)
    ssm_out = g * jax.nn.sigmoid(gate)
    na = _rms(a_ref[...], gattn_ref[...]).astype(BF16)
    ns = _rms(ssm_out, gssm_ref[...]).astype(BF16)
    aw = na.shape[1]
    m = (jnp.dot(na, wout_ref[:aw, :], preferred_element_type=F32)
         + jnp.dot(ns, wout_ref[aw:, :], preferred_element_type=F32))
    o_ref[...] = h_ref[...] + _rms(m, gpost_ref[...])


def _mix(h, attn, y, w_glu, g_attn, g_ssm, w_out, g_post, *, tm):
    m, d = h.shape
    aw = attn.shape[1]
    sw = y.shape[1]
    tm = min(tm, m)
    return pl.pallas_call(
        _mix_kernel,
        grid=(m // tm,),
        in_specs=[
            pl.BlockSpec((tm, d), lambda i: (i, 0)),
            pl.BlockSpec((tm, aw), lambda i: (i, 0)),
            pl.BlockSpec((tm, sw), lambda i: (i, 0)),
            pl.BlockSpec((sw, sw), lambda i: (0, 0)),
            pl.BlockSpec((1, aw), lambda i: (0, 0)),
            pl.BlockSpec((1, sw), lambda i: (0, 0)),
            pl.BlockSpec((d, d), lambda i: (0, 0)),
            pl.BlockSpec((1, d), lambda i: (0, 0)),
        ],
        out_specs=pl.BlockSpec((tm, d), lambda i: (i, 0)),
        out_shape=jax.ShapeDtypeStruct((m, d), F32),
        compiler_params=_params("parallel"),
        name="mix_out",
    )(h, attn, y, w_glu, g_attn, g_ssm, w_out, g_post)


FFN_TM = 512
FFN_TF = 512
PROJ_TM = 512
MIX_TM = 256
ATTN_TQ = 256
SSM_TC = 512


def kernel(x_prompt, x_sample, cache_k, cache_v, state_ssm_re, state_ssm_im, page_table, g_ffn1_pre, w_ffn1_gate, w_ffn1_up, w_ffn1_down, g_ffn1_post, g_mix_pre, w_in, sb_bias, lambda_re, lambda_im, log_dt, b_re, b_im, c_re, c_im, d_skip, w_glu, g_attn_out, g_ssm_out, w_out, g_mix_post, g_ffn2_pre, w_ffn2_gate, w_ffn2_up, w_ffn2_down, g_ffn2_post):
    depth = w_in.shape[0]
    bp, tp, d = x_prompt.shape
    bs, ts, _ = x_sample.shape
    groups, states = lambda_re.shape[1:]
    pool, page = cache_k.shape[1:3]
    width = cache_k.shape[3] * cache_k.shape[4]
    ck = cache_k.reshape(depth, pool, page, width)
    cv = cache_v.reshape(depth, pool, page, width)
    heads = width // HEAD_DIM

    hp = x_prompt.reshape(bp * tp, d)
    hs = x_sample.reshape(bs * ts, d)
    zero_state = jnp.zeros((bp, groups // GROUPS_PER_BLOCK, 1, 2 * GROUPS_PER_BLOCK * states), F32)
    row = lambda v: v.reshape(1, -1)
    outs = [[] for _ in range(8)]

    for l in range(depth):
        ffn1 = (row(g_ffn1_pre[l]), w_ffn1_gate[l].astype(BF16), w_ffn1_up[l].astype(BF16),
                w_ffn1_down[l].astype(BF16), row(g_ffn1_post[l]))
        ffn2 = (row(g_ffn2_pre[l]), w_ffn2_gate[l].astype(BF16), w_ffn2_up[l].astype(BF16),
                w_ffn2_down[l].astype(BF16), row(g_ffn2_post[l]))
        w_in_l = w_in[l].astype(BF16)
        mixw = (w_glu[l].astype(BF16), row(g_attn_out[l]), row(g_ssm_out[l]),
                w_out[l].astype(BF16), row(g_mix_post[l]))
        tables = _ssm_tables(lambda_re[l], lambda_im[l], log_dt[l], b_re[l], b_im[l],
                             c_re[l], c_im[l], d_skip[l])

        hp = _ffn(hp, *ffn1, tm=FFN_TM, tf=FFN_TF)
        hs = _ffn(hs, *ffn1, tm=FFN_TM, tf=FFN_TF)

        zp = _inproj(hp, row(g_mix_pre[l]), w_in_l, tm=PROJ_TM).reshape(4, bp, tp, width)
        zs = _inproj(hs, row(g_mix_pre[l]), w_in_l, tm=PROJ_TM).reshape(4, bs, ts, width)

        attn_p = _attn_prompt(zp, sb_bias[l], tq=ATTN_TQ)
        attn_s = _attn_sample(zs, ck, cv, l, page_table, sb_bias[l])

        yp, hlp = _ssm(zp, zero_state, tables, tc=SSM_TC)
        ys, hls = _ssm(zs, _pack_state(state_ssm_re[l], state_ssm_im[l]), tables, tc=SSM_TC)

        hp = _mix(hp, attn_p.reshape(bp * tp, width), yp.reshape(bp * tp, -1), *mixw, tm=MIX_TM)
        hs = _mix(hs, attn_s.reshape(bs * ts, width), ys.reshape(bs * ts, -1), *mixw, tm=MIX_TM)

        hp = _ffn(hp, *ffn2, tm=FFN_TM, tf=FFN_TF)
        hs = _ffn(hs, *ffn2, tm=FFN_TM, tf=FFN_TF)

        outs[0].append(zp[1].reshape(bp, tp, heads, HEAD_DIM))
        outs[1].append(zp[2].reshape(bp, tp, heads, HEAD_DIM))
        outs[2].append(zs[1].reshape(bs, ts, heads, HEAD_DIM))
        outs[3].append(zs[2].reshape(bs, ts, heads, HEAD_DIM))
        sp_re, sp_im = _unpack_state(hlp, groups, states)
        ss_re, ss_im = _unpack_state(hls, groups, states)
        outs[4].append(sp_re)
        outs[5].append(sp_im)
        outs[6].append(ss_re)
        outs[7].append(ss_im)

    return (hp.reshape(bp, tp, d), hs.reshape(bs, ts, d)) + tuple(jnp.stack(o) for o in outs)
```

```python
import functools
import math

import jax
import jax.numpy as jnp
from jax import lax
from jax.experimental import pallas as pl
from jax.experimental.pallas import tpu as pltpu

F32 = jnp.float32
BF16 = jnp.bfloat16

RMS_EPS = 1e-6
HEAD_DIM = 128
MXU_DIM = 256
SSM_GROUP = 16
SSM_STATE = 64
GROUPS_PER_BLOCK = 8
SUBLANES = 8
VMEM_LIMIT = 56 * 1024 * 1024


def _params(*sem):
    return pltpu.CompilerParams(dimension_semantics=sem, vmem_limit_bytes=VMEM_LIMIT)


def _rms(x, g):
    return x * lax.rsqrt(jnp.mean(x * x, axis=-1, keepdims=True) + RMS_EPS) * g


def _softplus(z):
    e = jnp.exp2(jnp.abs(z) * (-math.log2(math.e)))
    return jnp.maximum(z, 0.0) + jnp.log(1.0 + e)


def _nt_dot(a, b):
    return lax.dot_general(a, b, (((1,), (1,)), ((), ())), preferred_element_type=F32)


def _suffix_matrix(s):
    r = lax.broadcasted_iota(jnp.int32, (2 * s, s), 0)
    c = lax.broadcasted_iota(jnp.int32, (2 * s, s), 1)
    return jnp.where((r > c) & ((r < s) | (r > c + s)), 1.0, 0.0).astype(BF16)


def _sb_split(z, s, causal=None):
    m, w = z.shape
    n = w // s
    sp = _softplus(z)
    log_beta = z - sp
    if causal is not None:
        sp = jnp.where(causal, sp, 0.0)
    if n > 1:
        sp = jnp.concatenate([sp[:, i * s:(i + 1) * s] for i in range(n)], axis=0)
    hi = sp.astype(BF16)
    lo = (sp - hi.astype(F32)).astype(BF16)
    return log_beta, sp[:, :1], jnp.concatenate([hi, lo], axis=1)


def _sb_finish(log_beta, sp0, after, carry, causal=None):
    m, w = log_beta.shape
    n = after.shape[0] // m
    total = after[:, :1] + sp0
    parts = [None] * n
    for i in reversed(range(n)):
        parts[i] = after[i * m:(i + 1) * m] + carry
        carry = carry + total[i * m:(i + 1) * m]
    after = parts[0] if n == 1 else jnp.concatenate(parts, axis=1)
    wgt = jnp.exp(log_beta - after)
    if causal is not None:
        wgt = jnp.where(causal, wgt, 0.0)
    return wgt.astype(BF16), carry


def _ffn_kernel(x_ref, gpre_ref, wg_ref, wu_ref, wd_ref, gpost_ref, o_ref, n_ref, acc_ref):
    f = pl.program_id(1)

    @pl.when(f == 0)
    def _():
        n_ref[...] = _rms(x_ref[...], gpre_ref[...]).astype(BF16)
        acc_ref[...] = jnp.zeros_like(acc_ref)

    n = n_ref[...]
    gate = jnp.dot(n, wg_ref[...], preferred_element_type=F32)
    up = jnp.dot(n, wu_ref[...], preferred_element_type=F32)
    act = gate * jax.nn.sigmoid(gate) * up
    acc_ref[...] += jnp.dot(act.astype(BF16), wd_ref[...], preferred_element_type=F32)

    @pl.when(f == pl.num_programs(1) - 1)
    def _():
        o_ref[...] = x_ref[...] + 0.5 * _rms(acc_ref[...], gpost_ref[...])


def _ffn(x, g_pre, w_gate, w_up, w_down, g_post, *, tm, tf):
    m, d = x.shape
    dff = w_gate.shape[1]
    tm = min(tm, m)
    return pl.pallas_call(
        _ffn_kernel,
        grid=(m // tm, dff // tf),
        in_specs=[
            pl.BlockSpec((tm, d), lambda i, f: (i, 0)),
            pl.BlockSpec((1, d), lambda i, f: (0, 0)),
            pl.BlockSpec((d, tf), lambda i, f: (0, f)),
            pl.BlockSpec((d, tf), lambda i, f: (0, f)),
            pl.BlockSpec((tf, d), lambda i, f: (f, 0)),
            pl.BlockSpec((1, d), lambda i, f: (0, 0)),
        ],
        out_specs=pl.BlockSpec((tm, d), lambda i, f: (i, 0)),
        out_shape=jax.ShapeDtypeStruct((m, d), F32),
        scratch_shapes=[pltpu.VMEM((tm, d), BF16), pltpu.VMEM((tm, d), F32)],
        compiler_params=_params("parallel", "arbitrary"),
        name="ffn_half",
    )(x, g_pre, w_gate, w_up, w_down, g_post)


def _inproj_kernel(x_ref, g_ref, w_ref, kin_ref, vin_ref, qu_ref, k_ref, v_ref, n_ref):
    del kin_ref, vin_ref
    j = pl.program_id(1)

    @pl.when(j == 0)
    def _():
        n_ref[...] = _rms(x_ref[...], g_ref[...]).astype(BF16)

    res = jnp.dot(n_ref[...], w_ref[...], preferred_element_type=F32)

    @pl.when((j == 0) | (j == 3))
    def _():
        qu_ref[...] = res

    @pl.when(j == 1)
    def _():
        k_ref[...] = res

    @pl.when(j == 2)
    def _():
        v_ref[...] = res


def _inproj(x, g, w_in, k_all, v_all, layer, *, tm):
    m, d = x.shape
    width = w_in.shape[1] // 4
    tm = min(tm, m)
    return pl.pallas_call(
        _inproj_kernel,
        grid=(m // tm, 4),
        in_specs=[
            pl.BlockSpec((tm, d), lambda i, j: (i, 0)),
            pl.BlockSpec((1, d), lambda i, j: (0, 0)),
            pl.BlockSpec((d, width), lambda i, j: (0, j)),
            pl.BlockSpec(memory_space=pl.ANY),
            pl.BlockSpec(memory_space=pl.ANY),
        ],
        out_specs=[
            pl.BlockSpec((None, tm, width), lambda i, j: (j // 3, i, 0)),
            pl.BlockSpec((None, tm, width), lambda i, j: (layer, i, 0)),
            pl.BlockSpec((None, tm, width), lambda i, j: (layer, i, 0)),
        ],
        out_shape=[jax.ShapeDtypeStruct((2, m, width), F32),
                   jax.ShapeDtypeStruct(k_all.shape, F32),
                   jax.ShapeDtypeStruct(v_all.shape, F32)],
        input_output_aliases={3: 1, 4: 2},
        scratch_shapes=[pltpu.VMEM((tm, d), BF16)],
        compiler_params=_params("parallel", "arbitrary"),
        name="in_proj",
    )(x, g, w_in, k_all, v_all)


def _attn_kernel(bias_ref, q_ref, k_ref, v_ref, o_ref, *, tq, hps, scale):
    hb = pl.program_id(1)
    i = pl.program_id(2)
    sub = min(tq, MXU_DIM)
    suffix = _suffix_matrix(sub)
    row = lax.broadcasted_iota(jnp.int32, (tq, tq), 0)
    col = lax.broadcasted_iota(jnp.int32, (tq, tq), 1)
    lanes = [slice(n * HEAD_DIM, (n + 1) * HEAD_DIM) for n in range(hps)]
    qs = [q_ref[:, ln].astype(BF16) for ln in lanes]
    biases = [bias_ref[hb * hps + n] for n in range(hps)]

    def step(j, state, masked):
        start = pl.multiple_of(j * tq, tq)
        causal = (col < row) if masked else None
        zs = [_nt_dot(qs[n], k_ref[pl.ds(start, tq), ln].astype(BF16)) * scale + biases[n]
              for n, ln in enumerate(lanes)]
        split = [_sb_split(z, sub, causal) for z in zs]
        afters = [jnp.dot(st, suffix, preferred_element_type=F32) for _, _, st in split]
        new = []
        for n, ln in enumerate(lanes):
            carry, acc = state[n]
            w, carry = _sb_finish(split[n][0], split[n][1], afters[n], carry, causal)
            vb = v_ref[pl.ds(start, tq), ln].astype(BF16)
            new.append((carry, acc + jnp.dot(w, vb, preferred_element_type=F32)))
        return tuple(new)

    state = tuple((jnp.zeros((tq, 1), F32), jnp.zeros((tq, HEAD_DIM), F32)) for _ in lanes)
    state = step(i, state, True)
    state = lax.fori_loop(0, i, lambda t, s: step(i - 1 - t, s, False), state)
    for n, ln in enumerate(lanes):
        o_ref[:, ln] = state[n][1]


def _attn_prompt(qu, k_all, v_all, layer, sb_bias, *, tq, hps):
    _, b, t, width = qu.shape
    heads = width // HEAD_DIM
    tq = min(tq, t)
    hps = min(hps, heads)
    wide = hps * HEAD_DIM
    kern = functools.partial(_attn_kernel, tq=tq, hps=hps, scale=HEAD_DIM ** -0.5)
    return pl.pallas_call(
        kern,
        grid=(b, heads // hps, t // tq),
        in_specs=[
            pl.BlockSpec(memory_space=pltpu.SMEM),
            pl.BlockSpec((None, None, tq, wide), lambda bi, h, i: (0, bi, i, h)),
            pl.BlockSpec((None, None, t, wide), lambda bi, h, i: (layer, bi, 0, h)),
            pl.BlockSpec((None, None, t, wide), lambda bi, h, i: (layer, bi, 0, h)),
        ],
        out_specs=pl.BlockSpec((None, tq, wide), lambda bi, h, i: (bi, i, h)),
        out_shape=jax.ShapeDtypeStruct((b, t, width), F32),
        compiler_params=_params("parallel", "parallel", "arbitrary"),
        name="attn_prompt",
    )(sb_bias, qu, k_all, v_all)


def _sattn_kernel(pt_ref, qbd_ref, bias_ref, kn_ref, vn_ref, *rest,
                  scale, heads, tsteps, ppb):
    del pt_ref
    kc_refs, vc_refs = rest[:ppb], rest[ppb:2 * ppb]
    o_ref, acc_ref, carry_ref = rest[2 * ppb:]
    j = pl.program_id(1)
    rows = carry_ref.shape[0]
    page = kn_ref.shape[0]
    qbd = qbd_ref[...]

    def process(kb, vb, bias, causal):
        z = _nt_dot(qbd, kb) * scale + bias
        sub = min(z.shape[1], MXU_DIM)
        log_beta, sp0, stack = _sb_split(z, sub, causal)
        after = jnp.dot(stack, _suffix_matrix(sub), preferred_element_type=F32)
        w, carry = _sb_finish(log_beta, sp0, after, carry_ref[:, :1], causal)
        acc_ref[...] += jnp.dot(w, vb, preferred_element_type=F32)
        carry_ref[...] = jnp.broadcast_to(carry, carry_ref.shape)

    @pl.when(j == 0)
    def _():
        acc_ref[...] = jnp.zeros_like(acc_ref)
        carry_ref[...] = jnp.zeros_like(carry_ref)
        row = lax.broadcasted_iota(jnp.int32, (rows, page), 0)
        col = lax.broadcasted_iota(jnp.int32, (rows, page), 1)
        process(kn_ref[...].astype(BF16), vn_ref[...].astype(BF16), bias_ref[:, :page],
                col < (row % tsteps))

    def gather(ref):
        return jnp.concatenate(
            [ref[pl.ds(h, page, stride=heads), :].astype(BF16) for h in range(heads)], axis=1)

    kb = jnp.concatenate([gather(r) for r in kc_refs], axis=0)
    vb = jnp.concatenate([gather(r) for r in vc_refs], axis=0)
    process(kb, vb, bias_ref[...], None)

    @pl.when(j == pl.num_programs(1) - 1)
    def _():
        for h in range(heads):
            o_ref[:, h * HEAD_DIM:(h + 1) * HEAD_DIM] = (
                acc_ref[h * tsteps:(h + 1) * tsteps, h * HEAD_DIM:(h + 1) * HEAD_DIM])


def _attn_sample(q, k_new, v_new, cache_k, cache_v, layer, page_table, sb_bias, *, ppb):
    b, ts, width = q.shape
    heads = width // HEAD_DIM
    page = cache_k.shape[2] // heads
    n_pages = page_table.shape[1]
    ppb = min(ppb, n_pages)
    rows = heads * ts
    q = q.reshape(b, ts, heads, HEAD_DIM)
    eye = jnp.eye(heads, dtype=F32)
    qbd = (q.transpose(0, 2, 1, 3)[:, :, :, None, :] * eye[None, :, None, :, None])
    qbd = qbd.reshape(b, rows, width).astype(BF16)
    bias = jnp.broadcast_to(jnp.repeat(sb_bias, ts)[:, None], (rows, ppb * page)).astype(F32)
    pad = ((0, 0), (0, page - ts), (0, 0))
    k_new = jnp.pad(k_new, pad)
    v_new = jnp.pad(v_new, pad)
    kern = functools.partial(_sattn_kernel, scale=HEAD_DIM ** -0.5, heads=heads, tsteps=ts,
                             ppb=ppb)

    def page_spec(r):
        return pl.BlockSpec(
            (None, None, page * heads, HEAD_DIM),
            lambda bi, j, pt: (layer, pt[bi, n_pages - ppb * (j + 1) + r], 0, 0))

    return pl.pallas_call(
        kern,
        grid_spec=pltpu.PrefetchScalarGridSpec(
            num_scalar_prefetch=1,
            grid=(b, n_pages // ppb),
            in_specs=[
                pl.BlockSpec((None, rows, width), lambda bi, j, pt: (bi, 0, 0)),
                pl.BlockSpec((rows, ppb * page), lambda bi, j, pt: (0, 0)),
                pl.BlockSpec((None, page, width), lambda bi, j, pt: (bi, 0, 0)),
                pl.BlockSpec((None, page, width), lambda bi, j, pt: (bi, 0, 0)),
            ] + [page_spec(r) for r in range(ppb)] * 2,
            out_specs=pl.BlockSpec((None, ts, width), lambda bi, j, pt: (bi, 0, 0)),
            scratch_shapes=[pltpu.VMEM((rows, width), F32), pltpu.VMEM((rows, HEAD_DIM), F32)],
        ),
        out_shape=jax.ShapeDtypeStruct((b, ts, width), F32),
        compiler_params=_params("parallel", "arbitrary"),
        name="attn_sample",
    )(page_table, qbd, bias, k_new, v_new, *([cache_k] * ppb), *([cache_v] * ppb))


def _ssm_kernel(u_ref, h0_ref, bbd_ref, cbd_ref, d_ref, tbl_ref, y_ref, hl_ref, s_ref, carry_ref):
    t = pl.program_id(2)
    tc, lanes = s_ref.shape
    half = lanes // 2

    @pl.when(t == 0)
    def _():
        carry_ref[...] = jnp.broadcast_to(h0_ref[...], carry_ref.shape)

    u = u_ref[...]
    s_ref[...] = jnp.dot(u.astype(BF16), bbd_ref[...], preferred_element_type=F32)

    def tile(k, c):
        r0 = pl.multiple_of(k * SUBLANES, SUBLANES)
        xr = s_ref[pl.ds(r0, SUBLANES), :half]
        xi = s_ref[pl.ds(r0, SUBLANES), half:]
        for n, d in enumerate((1, 2, 4)):
            lr = tbl_ref[2 * n]
            li = tbl_ref[2 * n + 1]
            sr = pltpu.roll(xr, d, 0)
            si = pltpu.roll(xi, d, 0)
            xr, xi = xr + (lr * sr - li * si), xi + (lr * si + li * sr)
        pr = tbl_ref[6]
        pi = tbl_ref[7]
        cr, ci = c
        hr = xr + (pr * cr - pi * ci)
        hi = xi + (pr * ci + pi * cr)
        s_ref[pl.ds(r0, SUBLANES), :half] = hr
        s_ref[pl.ds(r0, SUBLANES), half:] = hi
        last = SUBLANES - 1
        return (jnp.broadcast_to(hr[last:, :], hr.shape), jnp.broadcast_to(hi[last:, :], hi.shape))

    cr, ci = lax.fori_loop(0, tc // SUBLANES, tile, (carry_ref[:, :half], carry_ref[:, half:]))
    carry_ref[:, :half] = cr
    carry_ref[:, half:] = ci
    y_ref[...] = (jnp.dot(s_ref[...].astype(BF16), cbd_ref[...], preferred_element_type=F32)
                  + d_ref[...] * u)

    @pl.when(t == pl.num_programs(2) - 1)
    def _():
        hl_ref[...] = carry_ref[:1, :]


def _ssm_tables(lam_re, lam_im, log_dt, b_re, b_im, c_re, c_im, d_skip):
    g, p = lam_re.shape
    c = b_re.shape[-1]
    nb = g // GROUPS_PER_BLOCK
    gb = GROUPS_PER_BLOCK
    lam = lax.complex(lam_re, lam_im)
    dt = jnp.exp(log_dt)[:, None]
    lam_bar = jnp.exp(lam * dt)
    b_bar = ((lam_bar - 1.0) / lam)[..., None] * lax.complex(b_re, b_im)
    eye = jnp.eye(gb, dtype=F32)

    def blockdiag_in(x):
        x = x.reshape(nb, gb, p, c).transpose(0, 1, 3, 2)
        x = x[:, :, :, None, :] * eye[None, :, None, :, None]
        return x.reshape(nb, gb * c, gb * p)

    bbd = jnp.concatenate([blockdiag_in(b_bar.real), blockdiag_in(b_bar.imag)], axis=-1)

    def blockdiag_out(x):
        x = x.reshape(nb, gb, c, p).transpose(0, 1, 3, 2)
        x = x[:, :, :, None, :] * eye[None, :, None, :, None]
        return x.reshape(nb, gb * p, gb * c)

    cbd = jnp.concatenate([blockdiag_out(c_re), -blockdiag_out(c_im)], axis=1)

    pw = [lam_bar]
    for _ in range(SUBLANES - 1):
        pw.append(pw[-1] * lam_bar)
    pw = jnp.stack(pw)
    rows = jnp.arange(SUBLANES)[:, None, None]
    tbls = []
    for d in (1, 2, 4):
        m = jnp.where(rows >= d, jnp.broadcast_to(pw[d - 1], pw.shape), 0.0)
        tbls += [m.real, m.imag]
    tbls += [pw.real, pw.imag]
    tbl = jnp.stack(tbls)
    tbl = tbl.reshape(8, SUBLANES, nb, gb * p).transpose(2, 0, 1, 3)
    dsk = d_skip.reshape(nb, 1, gb * c)
    return bbd.astype(BF16), cbd.astype(BF16), dsk, tbl


def _ssm(qu, h0, tables, *, tc):
    bbd, cbd, dsk, tbl = tables
    _, b, t, width = qu.shape
    nb, uw, sw = bbd.shape
    tc = min(tc, t)
    return pl.pallas_call(
        _ssm_kernel,
        grid=(b, nb, t // tc),
        in_specs=[
            pl.BlockSpec((None, None, tc, uw), lambda bi, g, ti: (1, bi, ti, g)),
            pl.BlockSpec((None, None, 1, sw), lambda bi, g, ti: (bi, g, 0, 0)),
            pl.BlockSpec((None, uw, sw), lambda bi, g, ti: (g, 0, 0)),
            pl.BlockSpec((None, sw, uw), lambda bi, g, ti: (g, 0, 0)),
            pl.BlockSpec((None, 1, uw), lambda bi, g, ti: (g, 0, 0)),
            pl.BlockSpec((None, 8, SUBLANES, sw // 2), lambda bi, g, ti: (g, 0, 0, 0)),
        ],
        out_specs=[
            pl.BlockSpec((None, tc, uw), lambda bi, g, ti: (bi, ti, g)),
            pl.BlockSpec((None, None, 1, sw), lambda bi, g, ti: (bi, g, 0, 0)),
        ],
        out_shape=[jax.ShapeDtypeStruct((b, t, width), F32),
                   jax.ShapeDtypeStruct((b, nb, 1, sw), F32)],
        scratch_shapes=[pltpu.VMEM((tc, sw), F32), pltpu.VMEM((SUBLANES, sw), F32)],
        compiler_params=_params("parallel", "parallel", "arbitrary"),
        name="s5_scan",
    )(qu, h0, bbd, cbd, dsk, tbl)


def _pack_state(re, im):
    b, g, p = re.shape
    nb = g // GROUPS_PER_BLOCK
    re = re.reshape(b, nb, 1, GROUPS_PER_BLOCK * p)
    im = im.reshape(b, nb, 1, GROUPS_PER_BLOCK * p)
    return jnp.concatenate([re, im], axis=-1)


def _unpack_state(hl, g, p):
    b = hl.shape[0]
    half = hl.shape[-1] // 2
    return hl[..., :half].reshape(b, g, p), hl[..., half:].reshape(b, g, p)


def _mix_kernel(h_ref, a_ref, y_ref, wglu_ref, gattn_ref, gssm_ref, wout_ref, gpost_ref, o_ref):
    y = y_ref[...]
    g = 0.5 * y * (1.0 + jnp.tanh(math.sqrt(2.0 / math.pi) * (y + 0.044715 * (y * y * y))))
    gate = jnp.dot(g.astype(BF16), wglu_ref[...], preferred_element_type=F32)
    ssm_out = g * jax.nn.sigmoid(gate)
    na = _rms(a_ref[...], gattn_ref[...]).astype(BF16)
    ns = _rms(ssm_out, gssm_ref[...]).astype(BF16)
    aw = na.shape[1]
    m = (jnp.dot(na, wout_ref[:aw, :], preferred_element_type=F32)
         + jnp.dot(ns, wout_ref[aw:, :], preferred_element_type=F32))
    o_ref[...] = h_ref[...] + _rms(m, gpost_ref[...])


def _mix(h, attn, y, w_glu, g_attn, g_ssm, w_out, g_post, *, tm):
    m, d = h.shape
    aw = attn.shape[1]
    sw = y.shape[1]
    tm = min(tm, m)
    return pl.pallas_call(
        _mix_kernel,
        grid=(m // tm,),
        in_specs=[
            pl.BlockSpec((tm, d), lambda i: (i, 0)),
            pl.BlockSpec((tm, aw), lambda i: (i, 0)),
            pl.BlockSpec((tm, sw), lambda i: (i, 0)),
            pl.BlockSpec((sw, sw), lambda i: (0, 0)),
            pl.BlockSpec((1, aw), lambda i: (0, 0)),
            pl.BlockSpec((1, sw), lambda i: (0, 0)),
            pl.BlockSpec((d, d), lambda i: (0, 0)),
            pl.BlockSpec((1, d), lambda i: (0, 0)),
        ],
        out_specs=pl.BlockSpec((tm, d), lambda i: (i, 0)),
        out_shape=jax.ShapeDtypeStruct((m, d), F32),
        compiler_params=_params("parallel"),
        name="mix_out",
    )(h, attn, y, w_glu, g_attn, g_ssm, w_out, g_post)


FFN_TM = 512
FFN_TF = 512
PROJ_TM = 512
MIX_TM = 256
ATTN_TQ = 256
ATTN_HEADS_PER_STEP = 4
PAGES_PER_STEP = 8
SSM_TC = 512


def kernel(x_prompt, x_sample, cache_k, cache_v, state_ssm_re, state_ssm_im, page_table, g_ffn1_pre, w_ffn1_gate, w_ffn1_up, w_ffn1_down, g_ffn1_post, g_mix_pre, w_in, sb_bias, lambda_re, lambda_im, log_dt, b_re, b_im, c_re, c_im, d_skip, w_glu, g_attn_out, g_ssm_out, w_out, g_mix_post, g_ffn2_pre, w_ffn2_gate, w_ffn2_up, w_ffn2_down, g_ffn2_post):
    depth = w_in.shape[0]
    bp, tp, d = x_prompt.shape
    bs, ts, _ = x_sample.shape
    groups, states = lambda_re.shape[1:]
    pool, page, heads = cache_k.shape[1:4]
    width = heads * HEAD_DIM
    ck = cache_k.reshape(depth, pool, page * heads, HEAD_DIM)
    cv = cache_v.reshape(depth, pool, page * heads, HEAD_DIM)

    hp = x_prompt.reshape(bp * tp, d)
    hs = x_sample.reshape(bs * ts, d)
    kp = jnp.zeros((depth, bp * tp, width), F32)
    vp = jnp.zeros((depth, bp * tp, width), F32)
    ks = jnp.zeros((depth, bs * ts, width), F32)
    vs = jnp.zeros((depth, bs * ts, width), F32)
    zero_state = jnp.zeros((bp, groups // GROUPS_PER_BLOCK, 1, 2 * GROUPS_PER_BLOCK * states), F32)
    row = lambda v: v.reshape(1, -1)
    states_out = [[] for _ in range(4)]

    for l in range(depth):
        ffn1 = (row(g_ffn1_pre[l]), w_ffn1_gate[l].astype(BF16), w_ffn1_up[l].astype(BF16),
                w_ffn1_down[l].astype(BF16), row(g_ffn1_post[l]))
        ffn2 = (row(g_ffn2_pre[l]), w_ffn2_gate[l].astype(BF16), w_ffn2_up[l].astype(BF16),
                w_ffn2_down[l].astype(BF16), row(g_ffn2_post[l]))
        w_in_l = w_in[l].astype(BF16)
        mixw = (w_glu[l].astype(BF16), row(g_attn_out[l]), row(g_ssm_out[l]),
                w_out[l].astype(BF16), row(g_mix_post[l]))
        tables = _ssm_tables(lambda_re[l], lambda_im[l], log_dt[l], b_re[l], b_im[l],
                             c_re[l], c_im[l], d_skip[l])

        hp = _ffn(hp, *ffn1, tm=FFN_TM, tf=FFN_TF)
        hs = _ffn(hs, *ffn1, tm=FFN_TM, tf=FFN_TF)

        qup, kp, vp = _inproj(hp, row(g_mix_pre[l]), w_in_l, kp, vp, l, tm=PROJ_TM)
        qus, ks, vs = _inproj(hs, row(g_mix_pre[l]), w_in_l, ks, vs, l, tm=PROJ_TM)
        qup = qup.reshape(2, bp, tp, width)
        qus = qus.reshape(2, bs, ts, width)

        attn_p = _attn_prompt(qup, kp.reshape(depth, bp, tp, width),
                              vp.reshape(depth, bp, tp, width), l, sb_bias[l],
                              tq=ATTN_TQ, hps=ATTN_HEADS_PER_STEP)
        attn_s = _attn_sample(qus[0], ks[l].reshape(bs, ts, width), vs[l].reshape(bs, ts, width),
                              ck, cv, l, page_table, sb_bias[l], ppb=PAGES_PER_STEP)

        yp, hlp = _ssm(qup, zero_state, tables, tc=SSM_TC)
        ys, hls = _ssm(qus, _pack_state(state_ssm_re[l], state_ssm_im[l]), tables, tc=SSM_TC)

        hp = _mix(hp, attn_p.reshape(bp * tp, width), yp.reshape(bp * tp, -1), *mixw, tm=MIX_TM)
        hs = _mix(hs, attn_s.reshape(bs * ts, width), ys.reshape(bs * ts, -1), *mixw, tm=MIX_TM)

        hp = _ffn(hp, *ffn2, tm=FFN_TM, tf=FFN_TF)
        hs = _ffn(hs, *ffn2, tm=FFN_TM, tf=FFN_TF)

        for dst, val in zip(states_out, _unpack_state(hlp, groups, states)
                            + _unpack_state(hls, groups, states)):
            dst.append(val)

    return (hp.reshape(bp, tp, d), hs.reshape(bs, ts, d),
            kp.reshape(depth, bp, tp, heads, HEAD_DIM), vp.reshape(depth, bp, tp, heads, HEAD_DIM),
            ks.reshape(depth, bs, ts, heads, HEAD_DIM), vs.reshape(depth, bs, ts, heads, HEAD_DIM),
            ) + tuple(jnp.stack(o) for o in states_out)
```

```python
import functools
import math

import jax
import jax.numpy as jnp
from jax import lax
from jax.experimental import pallas as pl
from jax.experimental.pallas import tpu as pltpu

F32 = jnp.float32
BF16 = jnp.bfloat16

RMS_EPS = 1e-6
HEAD_DIM = 128
MXU_DIM = 256
SSM_GROUP = 16
SSM_STATE = 64
GROUPS_PER_BLOCK = 8
SUBLANES = 8
VMEM_LIMIT = 60 * 1024 * 1024


def _params(*sem):
    return pltpu.CompilerParams(dimension_semantics=sem, vmem_limit_bytes=VMEM_LIMIT)


def _rms(x, g):
    return x * lax.rsqrt(jnp.mean(x * x, axis=-1, keepdims=True) + RMS_EPS) * g


def _softplus(z):
    e = jnp.exp2(jnp.abs(z) * (-math.log2(math.e)))
    return jnp.maximum(z, 0.0) + jnp.log(1.0 + e)


def _nt_dot(a, b):
    return lax.dot_general(a, b, (((1,), (1,)), ((), ())), preferred_element_type=F32)


def _suffix_matrix(s):
    r = lax.broadcasted_iota(jnp.int32, (2 * s, s), 0)
    c = lax.broadcasted_iota(jnp.int32, (2 * s, s), 1)
    return jnp.where((r > c) & ((r < s) | (r > c + s)), 1.0, 0.0).astype(BF16)


def _sb_split(z, s, causal=None):
    m, w = z.shape
    n = w // s
    sp = _softplus(z)
    log_beta = z - sp
    if causal is not None:
        sp = jnp.where(causal, sp, 0.0)
    if n > 1:
        sp = jnp.concatenate([sp[:, i * s:(i + 1) * s] for i in range(n)], axis=0)
    hi = sp.astype(BF16)
    lo = (sp - hi.astype(F32)).astype(BF16)
    return log_beta, sp[:, :1], jnp.concatenate([hi, lo], axis=1)


def _sb_finish(log_beta, sp0, after, carry, causal=None):
    m, w = log_beta.shape
    n = after.shape[0] // m
    total = after[:, :1] + sp0
    parts = [None] * n
    for i in reversed(range(n)):
        parts[i] = after[i * m:(i + 1) * m] + carry
        carry = carry + total[i * m:(i + 1) * m]
    after = parts[0] if n == 1 else jnp.concatenate(parts, axis=1)
    wgt = jnp.exp(log_beta - after)
    if causal is not None:
        wgt = jnp.where(causal, wgt, 0.0)
    return wgt.astype(BF16), carry


def _ffn_kernel(x_ref, gpre_ref, wg_ref, wu_ref, wd_ref, gpost_ref, o_ref, n_ref):
    f = pl.program_id(1)

    @pl.when(f == 0)
    def _():
        n_ref[...] = _rms(x_ref[...], gpre_ref[...]).astype(BF16)
        o_ref[...] = jnp.zeros_like(o_ref)

    n = n_ref[...]
    gate = jnp.dot(n, wg_ref[...].astype(BF16), preferred_element_type=F32)
    up = jnp.dot(n, wu_ref[...].astype(BF16), preferred_element_type=F32)
    act = gate * jax.nn.sigmoid(gate) * up
    o_ref[...] += jnp.dot(act.astype(BF16), wd_ref[...].astype(BF16), preferred_element_type=F32)

    @pl.when(f == pl.num_programs(1) - 1)
    def _():
        o_ref[...] = x_ref[...] + 0.5 * _rms(o_ref[...], gpost_ref[...])


def _ffn(x, g_pre, w_gate, w_up, w_down, g_post, layer, *, tm, tf):
    m, d = x.shape
    dff = w_gate.shape[2]
    tm = min(tm, m)
    return pl.pallas_call(
        _ffn_kernel,
        grid=(m // tm, dff // tf),
        in_specs=[
            pl.BlockSpec((tm, d), lambda i, f: (i, 0)),
            pl.BlockSpec((None, 1, d), lambda i, f: (layer, 0, 0)),
            pl.BlockSpec((None, d, tf), lambda i, f: (layer, 0, f)),
            pl.BlockSpec((None, d, tf), lambda i, f: (layer, 0, f)),
            pl.BlockSpec((None, tf, d), lambda i, f: (layer, f, 0)),
            pl.BlockSpec((None, 1, d), lambda i, f: (layer, 0, 0)),
        ],
        out_specs=pl.BlockSpec((tm, d), lambda i, f: (i, 0)),
        out_shape=jax.ShapeDtypeStruct((m, d), F32),
        scratch_shapes=[pltpu.VMEM((tm, d), BF16)],
        compiler_params=_params("parallel", "arbitrary"),
        name="ffn_half",
    )(x, g_pre, w_gate, w_up, w_down, g_post)


def _inproj_kernel(x_ref, g_ref, w_ref, kin_ref, vin_ref, qu_ref, k_ref, v_ref, n_ref):
    del kin_ref, vin_ref
    j = pl.program_id(1)

    @pl.when(j == 0)
    def _():
        n_ref[...] = _rms(x_ref[...], g_ref[...]).astype(BF16)

    res = jnp.dot(n_ref[...], w_ref[...], preferred_element_type=F32)

    @pl.when((j == 0) | (j == 3))
    def _():
        qu_ref[...] = res

    @pl.when(j == 1)
    def _():
        k_ref[...] = res

    @pl.when(j == 2)
    def _():
        v_ref[...] = res


def _inproj(x, g, w_in, k_all, v_all, layer, *, tm):
    m, d = x.shape
    width = w_in.shape[2] // 4
    tm = min(tm, m)
    return pl.pallas_call(
        _inproj_kernel,
        grid=(m // tm, 4),
        in_specs=[
            pl.BlockSpec((tm, d), lambda i, j: (i, 0)),
            pl.BlockSpec((None, 1, d), lambda i, j: (layer, 0, 0)),
            pl.BlockSpec((None, d, width), lambda i, j: (layer, 0, j)),
            pl.BlockSpec(memory_space=pl.ANY),
            pl.BlockSpec(memory_space=pl.ANY),
        ],
        out_specs=[
            pl.BlockSpec((None, tm, width), lambda i, j: (j // 3, i, 0)),
            pl.BlockSpec((None, tm, width), lambda i, j: (layer, i, 0)),
            pl.BlockSpec((None, tm, width), lambda i, j: (layer, i, 0)),
        ],
        out_shape=[jax.ShapeDtypeStruct((2, m, width), F32),
                   jax.ShapeDtypeStruct(k_all.shape, F32),
                   jax.ShapeDtypeStruct(v_all.shape, F32)],
        input_output_aliases={3: 1, 4: 2},
        scratch_shapes=[pltpu.VMEM((tm, d), BF16)],
        compiler_params=_params("parallel", "arbitrary"),
        name="in_proj",
    )(x, g, w_in, k_all, v_all)


def _attn_kernel(bias_ref, q_ref, k_ref, v_ref, o_ref, *, tq, hps, scale):
    hb = pl.program_id(1)
    i = pl.program_id(2)
    sub = min(tq, MXU_DIM)
    suffix = _suffix_matrix(sub)
    row = lax.broadcasted_iota(jnp.int32, (tq, tq), 0)
    col = lax.broadcasted_iota(jnp.int32, (tq, tq), 1)
    lanes = [slice(n * HEAD_DIM, (n + 1) * HEAD_DIM) for n in range(hps)]
    qs = [q_ref[:, ln].astype(BF16) for ln in lanes]
    biases = [bias_ref[hb * hps + n] for n in range(hps)]

    def step(j, state, masked):
        start = pl.multiple_of(j * tq, tq)
        causal = (col < row) if masked else None
        zs = [_nt_dot(qs[n], k_ref[pl.ds(start, tq), ln].astype(BF16)) * scale + biases[n]
              for n, ln in enumerate(lanes)]
        split = [_sb_split(z, sub, causal) for z in zs]
        afters = [jnp.dot(st, suffix, preferred_element_type=F32) for _, _, st in split]
        new = []
        for n, ln in enumerate(lanes):
            carry, acc = state[n]
            w, carry = _sb_finish(split[n][0], split[n][1], afters[n], carry, causal)
            vb = v_ref[pl.ds(start, tq), ln].astype(BF16)
            new.append((carry, acc + jnp.dot(w, vb, preferred_element_type=F32)))
        return tuple(new)

    state = tuple((jnp.zeros((tq, 1), F32), jnp.zeros((tq, HEAD_DIM), F32)) for _ in lanes)
    state = step(i, state, True)
    state = lax.fori_loop(0, i, lambda t, s: step(i - 1 - t, s, False), state)
    for n, ln in enumerate(lanes):
        o_ref[:, ln] = state[n][1]


def _attn_prompt(qu, k_all, v_all, layer, sb_bias, *, tq, hps):
    _, b, t, width = qu.shape
    heads = width // HEAD_DIM
    tq = min(tq, t)
    hps = min(hps, heads)
    wide = hps * HEAD_DIM
    kern = functools.partial(_attn_kernel, tq=tq, hps=hps, scale=HEAD_DIM ** -0.5)
    return pl.pallas_call(
        kern,
        grid=(b, heads // hps, t // tq),
        in_specs=[
            pl.BlockSpec(memory_space=pltpu.SMEM),
            pl.BlockSpec((None, None, tq, wide), lambda bi, h, i: (0, bi, i, h)),
            pl.BlockSpec((None, None, t, wide), lambda bi, h, i: (layer, bi, 0, h)),
            pl.BlockSpec((None, None, t, wide), lambda bi, h, i: (layer, bi, 0, h)),
        ],
        out_specs=pl.BlockSpec((None, tq, wide), lambda bi, h, i: (bi, i, h)),
        out_shape=jax.ShapeDtypeStruct((b, t, width), F32),
        compiler_params=_params("parallel", "parallel", "arbitrary"),
        name="attn_prompt",
    )(sb_bias, qu, k_all, v_all)


def _sattn_kernel(pt_ref, qbd_ref, bias_ref, kn_ref, vn_ref, *rest,
                  scale, heads, tsteps, ppb):
    del pt_ref
    kc_refs, vc_refs = rest[:ppb], rest[ppb:2 * ppb]
    o_ref, acc_ref, carry_ref = rest[2 * ppb:]
    j = pl.program_id(1)
    rows = carry_ref.shape[0]
    page = kn_ref.shape[0]
    qbd = qbd_ref[...]

    def process(kb, vb, bias, causal):
        z = _nt_dot(qbd, kb) * scale + bias
        sub = min(z.shape[1], MXU_DIM)
        log_beta, sp0, stack = _sb_split(z, sub, causal)
        after = jnp.dot(stack, _suffix_matrix(sub), preferred_element_type=F32)
        w, carry = _sb_finish(log_beta, sp0, after, carry_ref[:, :1], causal)
        acc_ref[...] += jnp.dot(w, vb, preferred_element_type=F32)
        carry_ref[...] = jnp.broadcast_to(carry, carry_ref.shape)

    @pl.when(j == 0)
    def _():
        acc_ref[...] = jnp.zeros_like(acc_ref)
        carry_ref[...] = jnp.zeros_like(carry_ref)
        row = lax.broadcasted_iota(jnp.int32, (rows, page), 0)
        col = lax.broadcasted_iota(jnp.int32, (rows, page), 1)
        process(kn_ref[...].astype(BF16), vn_ref[...].astype(BF16), bias_ref[:, :page],
                col < (row % tsteps))

    def gather(ref):
        return jnp.concatenate(
            [ref[pl.ds(h, page, stride=heads), :].astype(BF16) for h in range(heads)], axis=1)

    kb = jnp.concatenate([gather(r) for r in kc_refs], axis=0)
    vb = jnp.concatenate([gather(r) for r in vc_refs], axis=0)
    process(kb, vb, bias_ref[...], None)

    @pl.when(j == pl.num_programs(1) - 1)
    def _():
        for h in range(heads):
            o_ref[:, h * HEAD_DIM:(h + 1) * HEAD_DIM] = (
                acc_ref[h * tsteps:(h + 1) * tsteps, h * HEAD_DIM:(h + 1) * HEAD_DIM])


def _attn_sample(q, k_new, v_new, cache_k, cache_v, layer, page_table, sb_bias, *, ppb):
    b, ts, width = q.shape
    heads = width // HEAD_DIM
    page = cache_k.shape[2] // heads
    n_pages = page_table.shape[1]
    ppb = min(ppb, n_pages)
    rows = heads * ts
    q = q.reshape(b, ts, heads, HEAD_DIM)
    eye = jnp.eye(heads, dtype=F32)
    qbd = (q.transpose(0, 2, 1, 3)[:, :, :, None, :] * eye[None, :, None, :, None])
    qbd = qbd.reshape(b, rows, width).astype(BF16)
    bias = jnp.broadcast_to(jnp.repeat(sb_bias, ts)[:, None], (rows, ppb * page)).astype(F32)
    pad = ((0, 0), (0, page - ts), (0, 0))
    k_new = jnp.pad(k_new, pad)
    v_new = jnp.pad(v_new, pad)
    kern = functools.partial(_sattn_kernel, scale=HEAD_DIM ** -0.5, heads=heads, tsteps=ts,
                             ppb=ppb)

    def page_spec(r):
        return pl.BlockSpec(
            (None, None, page * heads, HEAD_DIM),
            lambda bi, j, pt: (layer, pt[bi, n_pages - ppb * (j + 1) + r], 0, 0))

    return pl.pallas_call(
        kern,
        grid_spec=pltpu.PrefetchScalarGridSpec(
            num_scalar_prefetch=1,
            grid=(b, n_pages // ppb),
            in_specs=[
                pl.BlockSpec((None, rows, width), lambda bi, j, pt: (bi, 0, 0)),
                pl.BlockSpec((rows, ppb * page), lambda bi, j, pt: (0, 0)),
                pl.BlockSpec((None, page, width), lambda bi, j, pt: (bi, 0, 0)),
                pl.BlockSpec((None, page, width), lambda bi, j, pt: (bi, 0, 0)),
            ] + [page_spec(r) for r in range(ppb)] * 2,
            out_specs=pl.BlockSpec((None, ts, width), lambda bi, j, pt: (bi, 0, 0)),
            scratch_shapes=[pltpu.VMEM((rows, width), F32), pltpu.VMEM((rows, HEAD_DIM), F32)],
        ),
        out_shape=jax.ShapeDtypeStruct((b, ts, width), F32),
        compiler_params=_params("parallel", "arbitrary"),
        name="attn_sample",
    )(page_table, qbd, bias, k_new, v_new, *([cache_k] * ppb), *([cache_v] * ppb))


def _cmul_add(xr, xi, ar, ai, br, bi):
    return xr + (ar * br - ai * bi), xi + (ar * bi + ai * br)


def _ssm_kernel(u_ref, h0_ref, bbd_ref, cbd_ref, d_ref, lam_ref, tbl_ref, y_ref, hl_ref,
                pad_ref, up_ref, s_ref, carry_ref):
    t = pl.program_id(2)
    tc, lanes = s_ref.shape
    half = lanes // 2
    seg = tc // SUBLANES
    rows8 = lambda k: pl.ds(pl.multiple_of(k * SUBLANES, SUBLANES), SUBLANES)

    @pl.when(t == 0)
    def _():
        carry_ref[...] = jnp.broadcast_to(h0_ref[...], carry_ref.shape)

    if seg > 1:
        pitch = seg + SUBLANES
        for j in range(SUBLANES):
            pad_ref[j * pitch:j * pitch + seg, :] = u_ref[j * seg:(j + 1) * seg, :]

        def permute(k, _):
            up_ref[rows8(k), :] = pad_ref[pl.ds(k, SUBLANES, stride=pitch), :]
            return _
        lax.fori_loop(0, seg, permute, 0)
        u = up_ref[...]
    else:
        u = u_ref[...]
    s_ref[...] = jnp.dot(u.astype(BF16), bbd_ref[...], preferred_element_type=F32)

    lam_r = jnp.broadcast_to(lam_ref[:, :half], (SUBLANES, half))
    lam_i = jnp.broadcast_to(lam_ref[:, half:], (SUBLANES, half))

    def local(k, c):
        return _cmul_add(s_ref[rows8(k), :half], s_ref[rows8(k), half:], lam_r, lam_i, *c)

    zero = jnp.zeros((SUBLANES, half), F32)
    xr, xi = lax.fori_loop(0, seg, local, (zero, zero))

    for n, d in enumerate((1, 2, 4)):
        xr, xi = _cmul_add(xr, xi, tbl_ref[2 * n], tbl_ref[2 * n + 1],
                           pltpu.roll(xr, d, 0), pltpu.roll(xi, d, 0))
    cr = carry_ref[:, :half]
    ci = carry_ref[:, half:]
    fr, fi = _cmul_add(xr, xi, tbl_ref[6], tbl_ref[7], cr, ci)
    first = lax.broadcasted_iota(jnp.int32, (SUBLANES, half), 0) == 0
    sr = jnp.where(first, cr, pltpu.roll(fr, 1, 0))
    si = jnp.where(first, ci, pltpu.roll(fi, 1, 0))
    last = SUBLANES - 1
    carry_ref[:, :half] = jnp.broadcast_to(fr[last:, :], (SUBLANES, half))
    carry_ref[:, half:] = jnp.broadcast_to(fi[last:, :], (SUBLANES, half))

    def true_scan(k, c):
        hr, hi = _cmul_add(s_ref[rows8(k), :half], s_ref[rows8(k), half:], lam_r, lam_i, *c)
        s_ref[rows8(k), :half] = hr
        s_ref[rows8(k), half:] = hi
        return hr, hi

    lax.fori_loop(0, seg, true_scan, (sr, si))

    y = (jnp.dot(s_ref[...].astype(BF16), cbd_ref[...], preferred_element_type=F32)
         + d_ref[...] * u)
    if seg > 1:
        up_ref[...] = y

        def unpermute(k, _):
            y_ref[pl.ds(k, SUBLANES, stride=seg), :] = up_ref[rows8(k), :]
            return _
        lax.fori_loop(0, seg, unpermute, 0)
    else:
        y_ref[...] = y

    @pl.when(t == pl.num_programs(2) - 1)
    def _():
        hl_ref[...] = carry_ref[:1, :]


def _ssm_tables(lam_re, lam_im, log_dt, b_re, b_im, c_re, c_im, d_skip, seg):
    g, p = lam_re.shape
    c = b_re.shape[-1]
    nb = g // GROUPS_PER_BLOCK
    gb = GROUPS_PER_BLOCK
    lam = lax.complex(lam_re, lam_im)
    dt = jnp.exp(log_dt)[:, None]
    lam_bar = jnp.exp(lam * dt)
    b_bar = ((lam_bar - 1.0) / lam)[..., None] * lax.complex(b_re, b_im)
    eye = jnp.eye(gb, dtype=F32)

    def blockdiag_in(x):
        x = x.reshape(nb, gb, p, c).transpose(0, 1, 3, 2)
        x = x[:, :, :, None, :] * eye[None, :, None, :, None]
        return x.reshape(nb, gb * c, gb * p)

    bbd = jnp.concatenate([blockdiag_in(b_bar.real), blockdiag_in(b_bar.imag)], axis=-1)

    def blockdiag_out(x):
        x = x.reshape(nb, gb, c, p).transpose(0, 1, 3, 2)
        x = x[:, :, :, None, :] * eye[None, :, None, :, None]
        return x.reshape(nb, gb * p, gb * c)

    cbd = jnp.concatenate([blockdiag_out(c_re), -blockdiag_out(c_im)], axis=1)

    def powers(base, n):
        out = [base]
        for _ in range(n - 1):
            out.append(out[-1] * base)
        return jnp.stack(out)

    lam_blk = lam_bar.reshape(nb, 1, gb * p)
    lam_blk = jnp.concatenate([lam_blk.real, lam_blk.imag], axis=-1)
    seg_pw = powers(powers(lam_bar, seg)[seg - 1], SUBLANES)
    rows = jnp.arange(SUBLANES)[:, None, None]
    tbls = []
    for d in (1, 2, 4):
        m = jnp.where(rows >= d, jnp.broadcast_to(seg_pw[d - 1], seg_pw.shape), 0.0)
        tbls += [m.real, m.imag]
    tbls += [seg_pw.real, seg_pw.imag]
    tbl = jnp.stack(tbls)
    tbl = tbl.reshape(8, SUBLANES, nb, gb * p).transpose(2, 0, 1, 3)
    dsk = d_skip.reshape(nb, 1, gb * c)
    return bbd.astype(BF16), cbd.astype(BF16), dsk, lam_blk, tbl


def _ssm(qu, h0, tables, *, tc):
    bbd, cbd, dsk, lam, tbl = tables
    _, b, t, width = qu.shape
    nb, uw, sw = bbd.shape
    assert t % tc == 0 and tc % SUBLANES == 0
    seg = tc // SUBLANES
    return pl.pallas_call(
        _ssm_kernel,
        grid=(b, nb, t // tc),
        in_specs=[
            pl.BlockSpec((None, None, tc, uw), lambda bi, g, ti: (1, bi, ti, g)),
            pl.BlockSpec((None, None, 1, sw), lambda bi, g, ti: (bi, g, 0, 0)),
            pl.BlockSpec((None, uw, sw), lambda bi, g, ti: (g, 0, 0)),
            pl.BlockSpec((None, sw, uw), lambda bi, g, ti: (g, 0, 0)),
            pl.BlockSpec((None, 1, uw), lambda bi, g, ti: (g, 0, 0)),
            pl.BlockSpec((None, 1, sw), lambda bi, g, ti: (g, 0, 0)),
            pl.BlockSpec((None, 8, SUBLANES, sw // 2), lambda bi, g, ti: (g, 0, 0, 0)),
        ],
        out_specs=[
            pl.BlockSpec((None, tc, uw), lambda bi, g, ti: (bi, ti, g)),
            pl.BlockSpec((None, None, 1, sw), lambda bi, g, ti: (bi, g, 0, 0)),
        ],
        out_shape=[jax.ShapeDtypeStruct((b, t, width), F32),
                   jax.ShapeDtypeStruct((b, nb, 1, sw), F32)],
        scratch_shapes=[pltpu.VMEM((SUBLANES * (seg + SUBLANES), uw), F32),
                        pltpu.VMEM((tc, uw), F32), pltpu.VMEM((tc, sw), F32),
                        pltpu.VMEM((SUBLANES, sw), F32)],
        compiler_params=_params("parallel", "parallel", "arbitrary"),
        name="s5_scan",
    )(qu, h0, bbd, cbd, dsk, lam, tbl)


def _pack_state(re, im):
    b, g, p = re.shape
    nb = g // GROUPS_PER_BLOCK
    re = re.reshape(b, nb, 1, GROUPS_PER_BLOCK * p)
    im = im.reshape(b, nb, 1, GROUPS_PER_BLOCK * p)
    return jnp.concatenate([re, im], axis=-1)


def _unpack_state(hl, g, p):
    b = hl.shape[0]
    half = hl.shape[-1] // 2
    return hl[..., :half].reshape(b, g, p), hl[..., half:].reshape(b, g, p)


def _mix_kernel(h_ref, a_ref, y_ref, wglu_ref, gattn_ref, gssm_ref, wout_ref, gpost_ref, o_ref):
    y = y_ref[...]
    g = 0.5 * y * (1.0 + jnp.tanh(math.sqrt(2.0 / math.pi) * (y + 0.044715 * (y * y * y))))
    gate = jnp.dot(g.astype(BF16), wglu_ref[...], preferred_element_type=F32)
    ssm_out = g * jax.nn.sigmoid(gate)
    na = _rms(a_ref[...], gattn_ref[...]).astype(BF16)
    ns = _rms(ssm_out, gssm_ref[...]).astype(BF16)
    aw = na.shape[1]
    m = (jnp.dot(na, wout_ref[:aw, :], preferred_element_type=F32)
         + jnp.dot(ns, wout_ref[aw:, :], preferred_element_type=F32))
    o_ref[...] = h_ref[...] + _rms(m, gpost_ref[...])


def _mix(h, attn, y, w_glu, g_attn, g_ssm, w_out, g_post, layer, *, tm):
    m, d = h.shape
    aw = attn.shape[1]
    sw = y.shape[1]
    tm = min(tm, m)
    return pl.pallas_call(
        _mix_kernel,
        grid=(m // tm,),
        in_specs=[
            pl.BlockSpec((tm, d), lambda i: (i, 0)),
            pl.BlockSpec((tm, aw), lambda i: (i, 0)),
            pl.BlockSpec((tm, sw), lambda i: (i, 0)),
            pl.BlockSpec((None, sw, sw), lambda i: (layer, 0, 0)),
            pl.BlockSpec((None, 1, aw), lambda i: (layer, 0, 0)),
            pl.BlockSpec((None, 1, sw), lambda i: (layer, 0, 0)),
            pl.BlockSpec((None, d, d), lambda i: (layer, 0, 0)),
            pl.BlockSpec((None, 1, d), lambda i: (layer, 0, 0)),
        ],
        out_specs=pl.BlockSpec((tm, d), lambda i: (i, 0)),
        out_shape=jax.ShapeDtypeStruct((m, d), F32),
        compiler_params=_params("parallel"),
        name="mix_out",
    )(h, attn, y, w_glu, g_attn, g_ssm, w_out, g_post)


FFN_TM = 1024
FFN_TF = 256
PROJ_TM = 512
MIX_TM = 256
ATTN_TQ = 256
ATTN_HEADS_PER_STEP = 4
PAGES_PER_STEP = 8
SSM_TC = 512


def kernel(x_prompt, x_sample, cache_k, cache_v, state_ssm_re, state_ssm_im, page_table, g_ffn1_pre, w_ffn1_gate, w_ffn1_up, w_ffn1_down, g_ffn1_post, g_mix_pre, w_in, sb_bias, lambda_re, lambda_im, log_dt, b_re, b_im, c_re, c_im, d_skip, w_glu, g_attn_out, g_ssm_out, w_out, g_mix_post, g_ffn2_pre, w_ffn2_gate, w_ffn2_up, w_ffn2_down, g_ffn2_post):
    depth = w_in.shape[0]
    bp, tp, d = x_prompt.shape
    bs, ts, _ = x_sample.shape
    groups, states = lambda_re.shape[1:]
    pool, page, heads = cache_k.shape[1:4]
    width = heads * HEAD_DIM
    ck = cache_k.reshape(depth, pool, page * heads, HEAD_DIM)
    cv = cache_v.reshape(depth, pool, page * heads, HEAD_DIM)

    hp = x_prompt.reshape(bp * tp, d)
    hs = x_sample.reshape(bs * ts, d)
    kp = jnp.zeros((depth, bp * tp, width), F32)
    vp = jnp.zeros((depth, bp * tp, width), F32)
    ks = jnp.zeros((depth, bs * ts, width), F32)
    vs = jnp.zeros((depth, bs * ts, width), F32)
    zero_state = jnp.zeros((bp, groups // GROUPS_PER_BLOCK, 1, 2 * GROUPS_PER_BLOCK * states), F32)
    states_out = [[] for _ in range(4)]
    tc_p = min(SSM_TC, tp)
    tc_s = min(SSM_TC, ts)

    gains = lambda g: g.reshape(depth, 1, -1)
    ffn1 = (gains(g_ffn1_pre), w_ffn1_gate, w_ffn1_up, w_ffn1_down, gains(g_ffn1_post))
    ffn2 = (gains(g_ffn2_pre), w_ffn2_gate, w_ffn2_up, w_ffn2_down, gains(g_ffn2_post))
    g_in = gains(g_mix_pre)
    w_in_b = w_in.astype(BF16)
    mixw = (w_glu.astype(BF16), gains(g_attn_out), gains(g_ssm_out), w_out.astype(BF16),
            gains(g_mix_post))

    for l in range(depth):
        s5 = (lambda_re[l], lambda_im[l], log_dt[l], b_re[l], b_im[l], c_re[l], c_im[l], d_skip[l])

        hp = _ffn(hp, *ffn1, l, tm=FFN_TM, tf=FFN_TF)
        hs = _ffn(hs, *ffn1, l, tm=FFN_TM, tf=FFN_TF)

        qup, kp, vp = _inproj(hp, g_in, w_in_b, kp, vp, l, tm=PROJ_TM)
        qus, ks, vs = _inproj(hs, g_in, w_in_b, ks, vs, l, tm=PROJ_TM)
        qup = qup.reshape(2, bp, tp, width)
        qus = qus.reshape(2, bs, ts, width)

        attn_p = _attn_prompt(qup, kp.reshape(depth, bp, tp, width),
                              vp.reshape(depth, bp, tp, width), l, sb_bias[l],
                              tq=ATTN_TQ, hps=ATTN_HEADS_PER_STEP)
        attn_s = _attn_sample(qus[0], ks[l].reshape(bs, ts, width), vs[l].reshape(bs, ts, width),
                              ck, cv, l, page_table, sb_bias[l], ppb=PAGES_PER_STEP)

        yp, hlp = _ssm(qup, zero_state, _ssm_tables(*s5, tc_p // SUBLANES), tc=tc_p)
        ys, hls = _ssm(qus, _pack_state(state_ssm_re[l], state_ssm_im[l]),
                       _ssm_tables(*s5, tc_s // SUBLANES), tc=tc_s)

        hp = _mix(hp, attn_p.reshape(bp * tp, width), yp.reshape(bp * tp, -1), *mixw, l, tm=MIX_TM)
        hs = _mix(hs, attn_s.reshape(bs * ts, width), ys.reshape(bs * ts, -1), *mixw, l, tm=MIX_TM)

        hp = _ffn(hp, *ffn2, l, tm=FFN_TM, tf=FFN_TF)
        hs = _ffn(hs, *ffn2, l, tm=FFN_TM, tf=FFN_TF)

        for dst, val in zip(states_out, _unpack_state(hlp, groups, states)
                            + _unpack_state(hls, groups, states)):
            dst.append(val)

    return (hp.reshape(bp, tp, d), hs.reshape(bs, ts, d),
            kp.reshape(depth, bp, tp, heads, HEAD_DIM), vp.reshape(depth, bp, tp, heads, HEAD_DIM),
            ks.reshape(depth, bs, ts, heads, HEAD_DIM), vs.reshape(depth, bs, ts, heads, HEAD_DIM),
            ) + tuple(jnp.stack(o) for o in states_out)
```

```python
import functools
import math

import jax
import jax.numpy as jnp
from jax import lax
from jax.experimental import pallas as pl
from jax.experimental.pallas import tpu as pltpu

F32 = jnp.float32
BF16 = jnp.bfloat16

RMS_EPS = 1e-6
HEAD_DIM = 128
MXU_DIM = 256
SSM_GROUP = 16
SSM_STATE = 64
GROUPS_PER_BLOCK = 8
SUBLANES = 8
VMEM_LIMIT = 60 * 1024 * 1024


def _params(*sem):
    return pltpu.CompilerParams(dimension_semantics=sem, vmem_limit_bytes=VMEM_LIMIT)


def _rms(x, g):
    return x * lax.rsqrt(jnp.mean(x * x, axis=-1, keepdims=True) + RMS_EPS) * g


LOG2E = math.log2(math.e)


def _softplus2(z2):
    return jnp.maximum(z2, 0.0) + jnp.log2(1.0 + jnp.exp2(-jnp.abs(z2)))


def _nt_dot(a, b):
    return lax.dot_general(a, b, (((1,), (1,)), ((), ())), preferred_element_type=F32)


def _suffix_matrix(s):
    r = lax.broadcasted_iota(jnp.int32, (2 * s, s), 0)
    c = lax.broadcasted_iota(jnp.int32, (2 * s, s), 1)
    return jnp.where((r > c) & ((r < s) | (r > c + s)), 1.0, 0.0).astype(BF16)


def _sb_split(z, s, causal=None):
    m, w = z.shape
    n = w // s
    sp = _softplus2(z)
    log_beta = z - sp
    if causal is not None:
        sp = jnp.where(causal, sp, 0.0)
    if n > 1:
        sp = jnp.concatenate([sp[:, i * s:(i + 1) * s] for i in range(n)], axis=0)
    hi = sp.astype(BF16)
    lo = (sp - hi.astype(F32)).astype(BF16)
    return log_beta, sp[:, :1], jnp.concatenate([hi, lo], axis=1)


def _sb_finish(log_beta, sp0, after, carry, causal=None):
    m, w = log_beta.shape
    n = after.shape[0] // m
    total = after[:, :1] + sp0
    parts = [None] * n
    for i in reversed(range(n)):
        parts[i] = after[i * m:(i + 1) * m] + carry
        carry = carry + total[i * m:(i + 1) * m]
    after = parts[0] if n == 1 else jnp.concatenate(parts, axis=1)
    wgt = jnp.exp2(log_beta - after)
    if causal is not None:
        wgt = jnp.where(causal, wgt, 0.0)
    return wgt.astype(BF16), carry


def _ffn_kernel(x_ref, gpre_ref, wg_ref, wu_ref, wd_ref, gpost_ref, o_ref, n_ref):
    f = pl.program_id(1)

    @pl.when(f == 0)
    def _():
        n_ref[...] = _rms(x_ref[...], gpre_ref[...]).astype(BF16)
        o_ref[...] = jnp.zeros_like(o_ref)

    n = n_ref[...]
    gate = jnp.dot(n, wg_ref[...].astype(BF16), preferred_element_type=F32)
    up = jnp.dot(n, wu_ref[...].astype(BF16), preferred_element_type=F32)
    act = gate * jax.nn.sigmoid(gate) * up
    o_ref[...] += jnp.dot(act.astype(BF16), wd_ref[...].astype(BF16), preferred_element_type=F32)

    @pl.when(f == pl.num_programs(1) - 1)
    def _():
        o_ref[...] = x_ref[...] + 0.5 * _rms(o_ref[...], gpost_ref[...])


def _ffn(x, g_pre, w_gate, w_up, w_down, g_post, layer, *, tm, tf):
    m, d = x.shape
    dff = w_gate.shape[2]
    tm = min(tm, m)
    return pl.pallas_call(
        _ffn_kernel,
        grid=(m // tm, dff // tf),
        in_specs=[
            pl.BlockSpec((tm, d), lambda i, f: (i, 0)),
            pl.BlockSpec((None, 1, d), lambda i, f: (layer, 0, 0)),
            pl.BlockSpec((None, d, tf), lambda i, f: (layer, 0, f)),
            pl.BlockSpec((None, d, tf), lambda i, f: (layer, 0, f)),
            pl.BlockSpec((None, tf, d), lambda i, f: (layer, f, 0)),
            pl.BlockSpec((None, 1, d), lambda i, f: (layer, 0, 0)),
        ],
        out_specs=pl.BlockSpec((tm, d), lambda i, f: (i, 0)),
        out_shape=jax.ShapeDtypeStruct((m, d), F32),
        scratch_shapes=[pltpu.VMEM((tm, d), BF16)],
        compiler_params=_params("parallel", "arbitrary"),
        name="ffn_half",
    )(x, g_pre, w_gate, w_up, w_down, g_post)


def _inproj_kernel(x_ref, g_ref, w_ref, kin_ref, vin_ref, qu_ref, k_ref, v_ref, qkv16_ref, n_ref,
                   *, q_scale):
    del kin_ref, vin_ref
    j = pl.program_id(1)

    @pl.when(j == 0)
    def _():
        n_ref[...] = _rms(x_ref[...], g_ref[...]).astype(BF16)

    res = jnp.dot(n_ref[...], w_ref[...], preferred_element_type=F32)

    @pl.when((j == 0) | (j == 3))
    def _():
        qu_ref[...] = res

    @pl.when(j == 0)
    def _():
        qkv16_ref[...] = (res * q_scale).astype(BF16)

    @pl.when(j == 1)
    def _():
        k_ref[...] = res
        qkv16_ref[...] = res.astype(BF16)

    @pl.when(j == 2)
    def _():
        v_ref[...] = res
        qkv16_ref[...] = res.astype(BF16)


def _inproj(x, g, w_in, k_all, v_all, layer, *, tm):
    m, d = x.shape
    width = w_in.shape[2] // 4
    tm = min(tm, m)
    return pl.pallas_call(
        functools.partial(_inproj_kernel, q_scale=HEAD_DIM ** -0.5 * LOG2E),
        grid=(m // tm, 4),
        in_specs=[
            pl.BlockSpec((tm, d), lambda i, j: (i, 0)),
            pl.BlockSpec((None, 1, d), lambda i, j: (layer, 0, 0)),
            pl.BlockSpec((None, d, width), lambda i, j: (layer, 0, j)),
            pl.BlockSpec(memory_space=pl.ANY),
            pl.BlockSpec(memory_space=pl.ANY),
        ],
        out_specs=[
            pl.BlockSpec((None, tm, width), lambda i, j: (j // 3, i, 0)),
            pl.BlockSpec((None, tm, width), lambda i, j: (layer, i, 0)),
            pl.BlockSpec((None, tm, width), lambda i, j: (layer, i, 0)),
            pl.BlockSpec((None, tm, width), lambda i, j: (jnp.minimum(j, 2), i, 0)),
        ],
        out_shape=[jax.ShapeDtypeStruct((2, m, width), F32),
                   jax.ShapeDtypeStruct(k_all.shape, F32),
                   jax.ShapeDtypeStruct(v_all.shape, F32),
                   jax.ShapeDtypeStruct((3, m, width), BF16)],
        input_output_aliases={3: 1, 4: 2},
        scratch_shapes=[pltpu.VMEM((tm, d), BF16)],
        compiler_params=_params("parallel", "arbitrary"),
        name="in_proj",
    )(x, g, w_in, k_all, v_all)


def _attn_kernel(bias_ref, q_ref, k_ref, v_ref, o_ref, *, tq, hps):
    hb = pl.program_id(1)
    i = pl.program_id(2)
    sub = min(tq, MXU_DIM)
    suffix = _suffix_matrix(sub)
    row = lax.broadcasted_iota(jnp.int32, (tq, tq), 0)
    col = lax.broadcasted_iota(jnp.int32, (tq, tq), 1)
    lanes = [slice(n * HEAD_DIM, (n + 1) * HEAD_DIM) for n in range(hps)]
    biases = [bias_ref[hb * hps + n] * LOG2E for n in range(hps)]

    def step(j, state, masked):
        keys = pl.ds(pl.multiple_of(j * tq, tq), tq)
        causal = (col < row) if masked else None
        zs = [_nt_dot(q_ref[:, ln], k_ref[keys, ln]) + biases[n] for n, ln in enumerate(lanes)]
        split = [_sb_split(z, sub, causal) for z in zs]
        afters = [jnp.dot(st, suffix, preferred_element_type=F32) for _, _, st in split]
        new = []
        for n, ln in enumerate(lanes):
            carry, acc = state[n]
            w, carry = _sb_finish(split[n][0], split[n][1], afters[n], carry, causal)
            new.append((carry, acc + jnp.dot(w, v_ref[keys, ln], preferred_element_type=F32)))
        return tuple(new)

    state = tuple((jnp.zeros((tq, 1), F32), jnp.zeros((tq, HEAD_DIM), F32)) for _ in lanes)
    state = step(i, state, True)
    state = lax.fori_loop(0, i, lambda t, s: step(i - 1 - t, s, False), state)
    for n, ln in enumerate(lanes):
        o_ref[:, ln] = state[n][1]


def _attn_prompt(qkv16, sb_bias, *, tq, hps):
    _, b, t, width = qkv16.shape
    heads = width // HEAD_DIM
    tq = min(tq, t)
    hps = min(hps, heads)
    wide = hps * HEAD_DIM
    return pl.pallas_call(
        functools.partial(_attn_kernel, tq=tq, hps=hps),
        grid=(b, heads // hps, t // tq),
        in_specs=[
            pl.BlockSpec(memory_space=pltpu.SMEM),
            pl.BlockSpec((None, None, tq, wide), lambda bi, h, i: (0, bi, i, h)),
            pl.BlockSpec((None, None, t, wide), lambda bi, h, i: (1, bi, 0, h)),
            pl.BlockSpec((None, None, t, wide), lambda bi, h, i: (2, bi, 0, h)),
        ],
        out_specs=pl.BlockSpec((None, tq, wide), lambda bi, h, i: (bi, i, h)),
        out_shape=jax.ShapeDtypeStruct((b, t, width), F32),
        compiler_params=_params("parallel", "parallel", "arbitrary"),
        name="attn_prompt",
    )(sb_bias, qkv16, qkv16, qkv16)


def _sattn_kernel(pt_ref, qbd_ref, bias_ref, kn_ref, vn_ref, *rest,
                  scale, heads, tsteps, ppb):
    del pt_ref
    kc_refs, vc_refs = rest[:ppb], rest[ppb:2 * ppb]
    o_ref, acc_ref, carry_ref = rest[2 * ppb:]
    j = pl.program_id(1)
    rows = carry_ref.shape[0]
    page = kn_ref.shape[0]
    qbd = qbd_ref[...]

    def process(kb, vb, bias, causal):
        z = _nt_dot(qbd, kb) * (scale * LOG2E) + bias
        sub = min(z.shape[1], MXU_DIM)
        log_beta, sp0, stack = _sb_split(z, sub, causal)
        after = jnp.dot(stack, _suffix_matrix(sub), preferred_element_type=F32)
        w, carry = _sb_finish(log_beta, sp0, after, carry_ref[:, :1], causal)
        acc_ref[...] += jnp.dot(w, vb, preferred_element_type=F32)
        carry_ref[...] = jnp.broadcast_to(carry, carry_ref.shape)

    @pl.when(j == 0)
    def _():
        acc_ref[...] = jnp.zeros_like(acc_ref)
        carry_ref[...] = jnp.zeros_like(carry_ref)
        row = lax.broadcasted_iota(jnp.int32, (rows, page), 0)
        col = lax.broadcasted_iota(jnp.int32, (rows, page), 1)
        process(kn_ref[...].astype(BF16), vn_ref[...].astype(BF16), bias_ref[:, :page],
                col < (row % tsteps))

    def gather(ref):
        return jnp.concatenate(
            [ref[pl.ds(h, page, stride=heads), :].astype(BF16) for h in range(heads)], axis=1)

    kb = jnp.concatenate([gather(r) for r in kc_refs], axis=0)
    vb = jnp.concatenate([gather(r) for r in vc_refs], axis=0)
    process(kb, vb, bias_ref[...], None)

    @pl.when(j == pl.num_programs(1) - 1)
    def _():
        for h in range(heads):
            o_ref[:, h * HEAD_DIM:(h + 1) * HEAD_DIM] = (
                acc_ref[h * tsteps:(h + 1) * tsteps, h * HEAD_DIM:(h + 1) * HEAD_DIM])


def _attn_sample(q, k_new, v_new, cache_k, cache_v, layer, page_table, sb_bias, *, ppb):
    b, ts, width = q.shape
    heads = width // HEAD_DIM
    page = cache_k.shape[2] // heads
    n_pages = page_table.shape[1]
    ppb = min(ppb, n_pages)
    rows = heads * ts
    q = q.reshape(b, ts, heads, HEAD_DIM)
    eye = jnp.eye(heads, dtype=F32)
    qbd = (q.transpose(0, 2, 1, 3)[:, :, :, None, :] * eye[None, :, None, :, None])
    qbd = qbd.reshape(b, rows, width).astype(BF16)
    bias = jnp.broadcast_to(jnp.repeat(sb_bias * LOG2E, ts)[:, None], (rows, ppb * page))
    pad = ((0, 0), (0, page - ts), (0, 0))
    k_new = jnp.pad(k_new, pad)
    v_new = jnp.pad(v_new, pad)
    kern = functools.partial(_sattn_kernel, scale=HEAD_DIM ** -0.5, heads=heads, tsteps=ts,
                             ppb=ppb)

    def page_spec(r):
        return pl.BlockSpec(
            (None, None, page * heads, HEAD_DIM),
            lambda bi, j, pt: (layer, pt[bi, n_pages - ppb * (j + 1) + r], 0, 0))

    return pl.pallas_call(
        kern,
        grid_spec=pltpu.PrefetchScalarGridSpec(
            num_scalar_prefetch=1,
            grid=(b, n_pages // ppb),
            in_specs=[
                pl.BlockSpec((None, rows, width), lambda bi, j, pt: (bi, 0, 0)),
                pl.BlockSpec((rows, ppb * page), lambda bi, j, pt: (0, 0)),
                pl.BlockSpec((None, page, width), lambda bi, j, pt: (bi, 0, 0)),
                pl.BlockSpec((None, page, width), lambda bi, j, pt: (bi, 0, 0)),
            ] + [page_spec(r) for r in range(ppb)] * 2,
            out_specs=pl.BlockSpec((None, ts, width), lambda bi, j, pt: (bi, 0, 0)),
            scratch_shapes=[pltpu.VMEM((rows, width), F32), pltpu.VMEM((rows, HEAD_DIM), F32)],
        ),
        out_shape=jax.ShapeDtypeStruct((b, ts, width), F32),
        compiler_params=_params("parallel", "arbitrary"),
        name="attn_sample",
    )(page_table, qbd, bias, k_new, v_new, *([cache_k] * ppb), *([cache_v] * ppb))


def _cmul_add(xr, xi, ar, ai, br, bi):
    return xr + (ar * br - ai * bi), xi + (ar * bi + ai * br)


def _ssm_kernel(u_ref, h0_ref, bbd_ref, cbd_ref, d_ref, lam_ref, tbl_ref, y_ref, hl_ref,
                pad_ref, up_ref, s_ref, carry_ref):
    t = pl.program_id(2)
    tc, lanes = s_ref.shape
    half = lanes // 2
    seg = tc // SUBLANES
    unroll = min(seg, 8)
    rows8 = lambda k: pl.ds(pl.multiple_of(k * SUBLANES, SUBLANES), SUBLANES)

    @pl.when(t == 0)
    def _():
        carry_ref[...] = jnp.broadcast_to(h0_ref[...], carry_ref.shape)

    if seg > 1:
        pitch = seg + SUBLANES
        for j in range(SUBLANES):
            pad_ref[j * pitch:j * pitch + seg, :] = u_ref[j * seg:(j + 1) * seg, :]

        def permute(k, _):
            up_ref[rows8(k), :] = pad_ref[pl.ds(k, SUBLANES, stride=pitch), :]
            return _
        lax.fori_loop(0, seg, permute, 0, unroll=unroll)
        u = up_ref[...]
    else:
        u = u_ref[...]
    s_ref[...] = jnp.dot(u.astype(BF16), bbd_ref[...], preferred_element_type=F32)

    lam_r = jnp.broadcast_to(lam_ref[:, :half], (SUBLANES, half))
    lam_i = jnp.broadcast_to(lam_ref[:, half:], (SUBLANES, half))

    def local(k, c):
        return _cmul_add(s_ref[rows8(k), :half], s_ref[rows8(k), half:], lam_r, lam_i, *c)

    zero = jnp.zeros((SUBLANES, half), F32)
    xr, xi = lax.fori_loop(0, seg, local, (zero, zero), unroll=unroll)

    for n, d in enumerate((1, 2, 4)):
        xr, xi = _cmul_add(xr, xi, tbl_ref[2 * n], tbl_ref[2 * n + 1],
                           pltpu.roll(xr, d, 0), pltpu.roll(xi, d, 0))
    cr = carry_ref[:, :half]
    ci = carry_ref[:, half:]
    fr, fi = _cmul_add(xr, xi, tbl_ref[6], tbl_ref[7], cr, ci)
    first = lax.broadcasted_iota(jnp.int32, (SUBLANES, half), 0) == 0
    sr = jnp.where(first, cr, pltpu.roll(fr, 1, 0))
    si = jnp.where(first, ci, pltpu.roll(fi, 1, 0))
    last = SUBLANES - 1
    carry_ref[:, :half] = jnp.broadcast_to(fr[last:, :], (SUBLANES, half))
    carry_ref[:, half:] = jnp.broadcast_to(fi[last:, :], (SUBLANES, half))

    def true_scan(k, c):
        hr, hi = _cmul_add(s_ref[rows8(k), :half], s_ref[rows8(k), half:], lam_r, lam_i, *c)
        s_ref[rows8(k), :half] = hr
        s_ref[rows8(k), half:] = hi
        return hr, hi

    lax.fori_loop(0, seg, true_scan, (sr, si), unroll=unroll)

    y = (jnp.dot(s_ref[...].astype(BF16), cbd_ref[...], preferred_element_type=F32)
         + d_ref[...] * u)
    if seg > 1:
        up_ref[...] = y

        def unpermute(k, _):
            y_ref[pl.ds(k, SUBLANES, stride=seg), :] = up_ref[rows8(k), :]
            return _
        lax.fori_loop(0, seg, unpermute, 0, unroll=unroll)
    else:
        y_ref[...] = y

    @pl.when(t == pl.num_programs(2) - 1)
    def _():
        hl_ref[...] = carry_ref[:1, :]


def _ssm_tables(lam_re, lam_im, log_dt, b_re, b_im, c_re, c_im, d_skip, seg):
    g, p = lam_re.shape
    c = b_re.shape[-1]
    nb = g // GROUPS_PER_BLOCK
    gb = GROUPS_PER_BLOCK
    lam = lax.complex(lam_re, lam_im)
    dt = jnp.exp(log_dt)[:, None]
    lam_bar = jnp.exp(lam * dt)
    b_bar = ((lam_bar - 1.0) / lam)[..., None] * lax.complex(b_re, b_im)
    eye = jnp.eye(gb, dtype=F32)

    def blockdiag_in(x):
        x = x.reshape(nb, gb, p, c).transpose(0, 1, 3, 2)
        x = x[:, :, :, None, :] * eye[None, :, None, :, None]
        return x.reshape(nb, gb * c, gb * p)

    bbd = jnp.concatenate([blockdiag_in(b_bar.real), blockdiag_in(b_bar.imag)], axis=-1)

    def blockdiag_out(x):
        x = x.reshape(nb, gb, c, p).transpose(0, 1, 3, 2)
        x = x[:, :, :, None, :] * eye[None, :, None, :, None]
        return x.reshape(nb, gb * p, gb * c)

    cbd = jnp.concatenate([blockdiag_out(c_re), -blockdiag_out(c_im)], axis=1)

    def powers(base, n):
        out = [base]
        for _ in range(n - 1):
            out.append(out[-1] * base)
        return jnp.stack(out)

    lam_blk = lam_bar.reshape(nb, 1, gb * p)
    lam_blk = jnp.concatenate([lam_blk.real, lam_blk.imag], axis=-1)
    seg_pw = powers(powers(lam_bar, seg)[seg - 1], SUBLANES)
    rows = jnp.arange(SUBLANES)[:, None, None]
    tbls = []
    for d in (1, 2, 4):
        m = jnp.where(rows >= d, jnp.broadcast_to(seg_pw[d - 1], seg_pw.shape), 0.0)
        tbls += [m.real, m.imag]
    tbls += [seg_pw.real, seg_pw.imag]
    tbl = jnp.stack(tbls)
    tbl = tbl.reshape(8, SUBLANES, nb, gb * p).transpose(2, 0, 1, 3)
    dsk = d_skip.reshape(nb, 1, gb * c)
    return bbd.astype(BF16), cbd.astype(BF16), dsk, lam_blk, tbl


def _ssm(qu, h0, tables, *, tc):
    bbd, cbd, dsk, lam, tbl = tables
    _, b, t, width = qu.shape
    nb, uw, sw = bbd.shape
    assert t % tc == 0 and tc % SUBLANES == 0
    seg = tc // SUBLANES
    return pl.pallas_call(
        _ssm_kernel,
        grid=(b, nb, t // tc),
        in_specs=[
            pl.BlockSpec((None, None, tc, uw), lambda bi, g, ti: (1, bi, ti, g)),
            pl.BlockSpec((None, None, 1, sw), lambda bi, g, ti: (bi, g, 0, 0)),
            pl.BlockSpec((None, uw, sw), lambda bi, g, ti: (g, 0, 0)),
            pl.BlockSpec((None, sw, uw), lambda bi, g, ti: (g, 0, 0)),
            pl.BlockSpec((None, 1, uw), lambda bi, g, ti: (g, 0, 0)),
            pl.BlockSpec((None, 1, sw), lambda bi, g, ti: (g, 0, 0)),
            pl.BlockSpec((None, 8, SUBLANES, sw // 2), lambda bi, g, ti: (g, 0, 0, 0)),
        ],
        out_specs=[
            pl.BlockSpec((None, tc, uw), lambda bi, g, ti: (bi, ti, g)),
            pl.BlockSpec((None, None, 1, sw), lambda bi, g, ti: (bi, g, 0, 0)),
        ],
        out_shape=[jax.ShapeDtypeStruct((b, t, width), F32),
                   jax.ShapeDtypeStruct((b, nb, 1, sw), F32)],
        scratch_shapes=[pltpu.VMEM((SUBLANES * (seg + SUBLANES), uw), F32),
                        pltpu.VMEM((tc, uw), F32), pltpu.VMEM((tc, sw), F32),
                        pltpu.VMEM((SUBLANES, sw), F32)],
        compiler_params=_params("parallel", "parallel", "arbitrary"),
        name="s5_scan",
    )(qu, h0, bbd, cbd, dsk, lam, tbl)


def _pack_state(re, im):
    b, g, p = re.shape
    nb = g // GROUPS_PER_BLOCK
    re = re.reshape(b, nb, 1, GROUPS_PER_BLOCK * p)
    im = im.reshape(b, nb, 1, GROUPS_PER_BLOCK * p)
    return jnp.concatenate([re, im], axis=-1)


def _unpack_state(hl, g, p):
    b = hl.shape[0]
    half = hl.shape[-1] // 2
    return hl[..., :half].reshape(b, g, p), hl[..., half:].reshape(b, g, p)


def _mix_kernel(h_ref, a_ref, y_ref, wglu_ref, gattn_ref, gssm_ref, wout_ref, gpost_ref, o_ref):
    y = y_ref[...]
    g = 0.5 * y * (1.0 + jnp.tanh(math.sqrt(2.0 / math.pi) * (y + 0.044715 * (y * y * y))))
    gate = jnp.dot(g.astype(BF16), wglu_ref[...], preferred_element_type=F32)
    ssm_out = g * jax.nn.sigmoid(gate)
    na = _rms(a_ref[...], gattn_ref[...]).astype(BF16)
    ns = _rms(ssm_out, gssm_ref[...]).astype(BF16)
    aw = na.shape[1]
    m = (jnp.dot(na, wout_ref[:aw, :], preferred_element_type=F32)
         + jnp.dot(ns, wout_ref[aw:, :], preferred_element_type=F32))
    o_ref[...] = h_ref[...] + _rms(m, gpost_ref[...])


def _mix(h, attn, y, w_glu, g_attn, g_ssm, w_out, g_post, layer, *, tm):
    m, d = h.shape
    aw = attn.shape[1]
    sw = y.shape[1]
    tm = min(tm, m)
    return pl.pallas_call(
        _mix_kernel,
        grid=(m // tm,),
        in_specs=[
            pl.BlockSpec((tm, d), lambda i: (i, 0)),
            pl.BlockSpec((tm, aw), lambda i: (i, 0)),
            pl.BlockSpec((tm, sw), lambda i: (i, 0)),
            pl.BlockSpec((None, sw, sw), lambda i: (layer, 0, 0)),
            pl.BlockSpec((None, 1, aw), lambda i: (layer, 0, 0)),
            pl.BlockSpec((None, 1, sw), lambda i: (layer, 0, 0)),
            pl.BlockSpec((None, d, d), lambda i: (layer, 0, 0)),
            pl.BlockSpec((None, 1, d), lambda i: (layer, 0, 0)),
        ],
        out_specs=pl.BlockSpec((tm, d), lambda i: (i, 0)),
        out_shape=jax.ShapeDtypeStruct((m, d), F32),
        compiler_params=_params("parallel"),
        name="mix_out",
    )(h, attn, y, w_glu, g_attn, g_ssm, w_out, g_post)


FFN_TM = 1024
FFN_TF = 256
PROJ_TM = 512
MIX_TM = 256
ATTN_TQ = 256
ATTN_HEADS_PER_STEP = 8
PAGES_PER_STEP = 8
SSM_TC = 512


def kernel(x_prompt, x_sample, cache_k, cache_v, state_ssm_re, state_ssm_im, page_table, g_ffn1_pre, w_ffn1_gate, w_ffn1_up, w_ffn1_down, g_ffn1_post, g_mix_pre, w_in, sb_bias, lambda_re, lambda_im, log_dt, b_re, b_im, c_re, c_im, d_skip, w_glu, g_attn_out, g_ssm_out, w_out, g_mix_post, g_ffn2_pre, w_ffn2_gate, w_ffn2_up, w_ffn2_down, g_ffn2_post):
    depth = w_in.shape[0]
    bp, tp, d = x_prompt.shape
    bs, ts, _ = x_sample.shape
    groups, states = lambda_re.shape[1:]
    pool, page, heads = cache_k.shape[1:4]
    width = heads * HEAD_DIM
    ck = cache_k.reshape(depth, pool, page * heads, HEAD_DIM)
    cv = cache_v.reshape(depth, pool, page * heads, HEAD_DIM)

    hp = x_prompt.reshape(bp * tp, d)
    hs = x_sample.reshape(bs * ts, d)
    kp = jnp.zeros((depth, bp * tp, width), F32)
    vp = jnp.zeros((depth, bp * tp, width), F32)
    ks = jnp.zeros((depth, bs * ts, width), F32)
    vs = jnp.zeros((depth, bs * ts, width), F32)
    zero_state = jnp.zeros((bp, groups // GROUPS_PER_BLOCK, 1, 2 * GROUPS_PER_BLOCK * states), F32)
    states_out = [[] for _ in range(4)]
    tc_p = min(SSM_TC, tp)
    tc_s = min(SSM_TC, ts)

    gains = lambda g: g.reshape(depth, 1, -1)
    ffn1 = (gains(g_ffn1_pre), w_ffn1_gate, w_ffn1_up, w_ffn1_down, gains(g_ffn1_post))
    ffn2 = (gains(g_ffn2_pre), w_ffn2_gate, w_ffn2_up, w_ffn2_down, gains(g_ffn2_post))
    g_in = gains(g_mix_pre)
    w_in_b = w_in.astype(BF16)
    mixw = (w_glu.astype(BF16), gains(g_attn_out), gains(g_ssm_out), w_out.astype(BF16),
            gains(g_mix_post))

    for l in range(depth):
        s5 = (lambda_re[l], lambda_im[l], log_dt[l], b_re[l], b_im[l], c_re[l], c_im[l], d_skip[l])

        hp = _ffn(hp, *ffn1, l, tm=FFN_TM, tf=FFN_TF)
        hs = _ffn(hs, *ffn1, l, tm=FFN_TM, tf=FFN_TF)

        qup, kp, vp, qkv16 = _inproj(hp, g_in, w_in_b, kp, vp, l, tm=PROJ_TM)
        qus, ks, vs, _ = _inproj(hs, g_in, w_in_b, ks, vs, l, tm=PROJ_TM)
        qup = qup.reshape(2, bp, tp, width)
        qus = qus.reshape(2, bs, ts, width)

        attn_p = _attn_prompt(qkv16.reshape(3, bp, tp, width), sb_bias[l],
                              tq=ATTN_TQ, hps=ATTN_HEADS_PER_STEP)
        attn_s = _attn_sample(qus[0], ks[l].reshape(bs, ts, width), vs[l].reshape(bs, ts, width),
                              ck, cv, l, page_table, sb_bias[l], ppb=PAGES_PER_STEP)

        yp, hlp = _ssm(qup, zero_state, _ssm_tables(*s5, tc_p // SUBLANES), tc=tc_p)
        ys, hls = _ssm(qus, _pack_state(state_ssm_re[l], state_ssm_im[l]),
                       _ssm_tables(*s5, tc_s // SUBLANES), tc=tc_s)

        hp = _mix(hp, attn_p.reshape(bp * tp, width), yp.reshape(bp * tp, -1), *mixw, l, tm=MIX_TM)
        hs = _mix(hs, attn_s.reshape(bs * ts, width), ys.reshape(bs * ts, -1), *mixw, l, tm=MIX_TM)

        hp = _ffn(hp, *ffn2, l, tm=FFN_TM, tf=FFN_TF)
        hs = _ffn(hs, *ffn2, l, tm=FFN_TM, tf=FFN_TF)

        for dst, val in zip(states_out, _unpack_state(hlp, groups, states)
                            + _unpack_state(hls, groups, states)):
            dst.append(val)

    return (hp.reshape(bp, tp, d), hs.reshape(bs, ts, d),
            kp.reshape(depth, bp, tp, heads, HEAD_DIM), vp.reshape(depth, bp, tp, heads, HEAD_DIM),
            ks.reshape(depth, bs, ts, heads, HEAD_DIM), vs.reshape(depth, bs, ts, heads, HEAD_DIM),
            ) + tuple(jnp.stack(o) for o in states_out)
```

```python
import functools
import math

import jax
import jax.numpy as jnp
from jax import lax
from jax.experimental import pallas as pl
from jax.experimental.pallas import tpu as pltpu

F32 = jnp.float32
BF16 = jnp.bfloat16

RMS_EPS = 1e-6
HEAD_DIM = 128
MXU_DIM = 256
SSM_GROUP = 16
SSM_STATE = 64
GROUPS_PER_BLOCK = 8
SUBLANES = 8
VMEM_LIMIT = 60 * 1024 * 1024


def _params(*sem):
    return pltpu.CompilerParams(dimension_semantics=sem, vmem_limit_bytes=VMEM_LIMIT)


def _rms(x, g):
    return x * lax.rsqrt(jnp.mean(x * x, axis=-1, keepdims=True) + RMS_EPS) * g


LOG2E = math.log2(math.e)


def _softplus2(z2):
    return jnp.maximum(z2, 0.0) + jnp.log2(1.0 + jnp.exp2(-jnp.abs(z2)))


def _nt_dot(a, b):
    return lax.dot_general(a, b, (((1,), (1,)), ((), ())), preferred_element_type=F32)


def _suffix_matrix(s):
    r = lax.broadcasted_iota(jnp.int32, (2 * s, s), 0)
    c = lax.broadcasted_iota(jnp.int32, (2 * s, s), 1)
    return jnp.where((r > c) & ((r < s) | (r > c + s)), 1.0, 0.0).astype(BF16)


def _sb_split(z, s, causal=None):
    m, w = z.shape
    n = w // s
    sp = _softplus2(z)
    log_beta = z - sp
    if causal is not None:
        sp = jnp.where(causal, sp, 0.0)
    if n > 1:
        sp = jnp.concatenate([sp[:, i * s:(i + 1) * s] for i in range(n)], axis=0)
    hi = sp.astype(BF16)
    lo = (sp - hi.astype(F32)).astype(BF16)
    return log_beta, sp[:, :1], jnp.concatenate([hi, lo], axis=1)


def _sb_finish(log_beta, sp0, after, carry, causal=None):
    m, w = log_beta.shape
    n = after.shape[0] // m
    total = after[:, :1] + sp0
    parts = [None] * n
    for i in reversed(range(n)):
        parts[i] = after[i * m:(i + 1) * m] + carry
        carry = carry + total[i * m:(i + 1) * m]
    after = parts[0] if n == 1 else jnp.concatenate(parts, axis=1)
    wgt = jnp.exp2(log_beta - after)
    if causal is not None:
        wgt = jnp.where(causal, wgt, 0.0)
    return wgt.astype(BF16), carry


def _ffn_kernel(x_ref, gpre_ref, wg_ref, wu_ref, wd_ref, gpost_ref, o_ref, n_ref):
    f = pl.program_id(1)

    @pl.when(f == 0)
    def _():
        n_ref[...] = _rms(x_ref[...], gpre_ref[...]).astype(BF16)
        o_ref[...] = jnp.zeros_like(o_ref)

    n = n_ref[...]
    gate = jnp.dot(n, wg_ref[...].astype(BF16), preferred_element_type=F32)
    up = jnp.dot(n, wu_ref[...].astype(BF16), preferred_element_type=F32)
    act = gate * jax.nn.sigmoid(gate) * up
    o_ref[...] += jnp.dot(act.astype(BF16), wd_ref[...].astype(BF16), preferred_element_type=F32)

    @pl.when(f == pl.num_programs(1) - 1)
    def _():
        o_ref[...] = x_ref[...] + 0.5 * _rms(o_ref[...], gpost_ref[...])


def _ffn(x, g_pre, w_gate, w_up, w_down, g_post, layer, *, tm, tf):
    m, d = x.shape
    dff = w_gate.shape[2]
    tm = min(tm, m)
    return pl.pallas_call(
        _ffn_kernel,
        grid=(m // tm, dff // tf),
        in_specs=[
            pl.BlockSpec((tm, d), lambda i, f: (i, 0)),
            pl.BlockSpec((None, 1, d), lambda i, f: (layer, 0, 0)),
            pl.BlockSpec((None, d, tf), lambda i, f: (layer, 0, f)),
            pl.BlockSpec((None, d, tf), lambda i, f: (layer, 0, f)),
            pl.BlockSpec((None, tf, d), lambda i, f: (layer, f, 0)),
            pl.BlockSpec((None, 1, d), lambda i, f: (layer, 0, 0)),
        ],
        out_specs=pl.BlockSpec((tm, d), lambda i, f: (i, 0)),
        out_shape=jax.ShapeDtypeStruct((m, d), F32),
        scratch_shapes=[pltpu.VMEM((tm, d), BF16)],
        compiler_params=_params("parallel", "arbitrary"),
        name="ffn_half",
    )(x, g_pre, w_gate, w_up, w_down, g_post)


def _inproj_kernel(x_ref, g_ref, w_ref, kin_ref, vin_ref, qu_ref, k_ref, v_ref, qkv16_ref, n_ref,
                   *, q_scale):
    del kin_ref, vin_ref
    j = pl.program_id(1)

    @pl.when(j == 0)
    def _():
        n_ref[...] = _rms(x_ref[...], g_ref[...]).astype(BF16)

    res = jnp.dot(n_ref[...], w_ref[...], preferred_element_type=F32)

    @pl.when((j == 0) | (j == 3))
    def _():
        qu_ref[...] = res

    @pl.when(j == 0)
    def _():
        qkv16_ref[...] = (res * q_scale).astype(BF16)

    @pl.when(j == 1)
    def _():
        k_ref[...] = res
        qkv16_ref[...] = res.astype(BF16)

    @pl.when(j == 2)
    def _():
        v_ref[...] = res
        qkv16_ref[...] = res.astype(BF16)


def _inproj(x, g, w_in, k_all, v_all, layer, *, tm):
    m, d = x.shape
    width = w_in.shape[2] // 4
    tm = min(tm, m)
    return pl.pallas_call(
        functools.partial(_inproj_kernel, q_scale=HEAD_DIM ** -0.5 * LOG2E),
        grid=(m // tm, 4),
        in_specs=[
            pl.BlockSpec((tm, d), lambda i, j: (i, 0)),
            pl.BlockSpec((None, 1, d), lambda i, j: (layer, 0, 0)),
            pl.BlockSpec((None, d, width), lambda i, j: (layer, 0, j)),
            pl.BlockSpec(memory_space=pl.ANY),
            pl.BlockSpec(memory_space=pl.ANY),
        ],
        out_specs=[
            pl.BlockSpec((None, tm, width), lambda i, j: (j // 3, i, 0)),
            pl.BlockSpec((None, tm, width), lambda i, j: (layer, i, 0)),
            pl.BlockSpec((None, tm, width), lambda i, j: (layer, i, 0)),
            pl.BlockSpec((None, tm, width), lambda i, j: (jnp.minimum(j, 2), i, 0)),
        ],
        out_shape=[jax.ShapeDtypeStruct((2, m, width), F32),
                   jax.ShapeDtypeStruct(k_all.shape, F32),
                   jax.ShapeDtypeStruct(v_all.shape, F32),
                   jax.ShapeDtypeStruct((3, m, width), BF16)],
        input_output_aliases={3: 1, 4: 2},
        scratch_shapes=[pltpu.VMEM((tm, d), BF16)],
        compiler_params=_params("parallel", "arbitrary"),
        name="in_proj",
    )(x, g, w_in, k_all, v_all)


def _attn_kernel(bias_ref, q_ref, k_ref, v_ref, o_ref, *, tq, hps):
    hb = pl.program_id(1)
    i = pl.program_id(2)
    sub = min(tq, MXU_DIM)
    suffix = _suffix_matrix(sub)
    row = lax.broadcasted_iota(jnp.int32, (tq, tq), 0)
    col = lax.broadcasted_iota(jnp.int32, (tq, tq), 1)
    lanes = [slice(n * HEAD_DIM, (n + 1) * HEAD_DIM) for n in range(hps)]
    biases = [bias_ref[hb * hps + n] * LOG2E for n in range(hps)]

    def step(j, state, masked):
        keys = pl.ds(pl.multiple_of(j * tq, tq), tq)
        causal = (col < row) if masked else None
        zs = [_nt_dot(q_ref[:, ln], k_ref[keys, ln]) + biases[n] for n, ln in enumerate(lanes)]
        split = [_sb_split(z, sub, causal) for z in zs]
        afters = [jnp.dot(st, suffix, preferred_element_type=F32) for _, _, st in split]
        new = []
        for n, ln in enumerate(lanes):
            carry, acc = state[n]
            w, carry = _sb_finish(split[n][0], split[n][1], afters[n], carry, causal)
            new.append((carry, acc + jnp.dot(w, v_ref[keys, ln], preferred_element_type=F32)))
        return tuple(new)

    state = tuple((jnp.zeros((tq, 1), F32), jnp.zeros((tq, HEAD_DIM), F32)) for _ in lanes)
    state = step(i, state, True)
    state = lax.fori_loop(0, i, lambda t, s: step(i - 1 - t, s, False), state)
    for n, ln in enumerate(lanes):
        o_ref[:, ln] = state[n][1]


def _attn_prompt(qkv16, sb_bias, *, tq, hps):
    _, b, t, width = qkv16.shape
    heads = width // HEAD_DIM
    tq = min(tq, t)
    hps = min(hps, heads)
    wide = hps * HEAD_DIM
    return pl.pallas_call(
        functools.partial(_attn_kernel, tq=tq, hps=hps),
        grid=(b, heads // hps, t // tq),
        in_specs=[
            pl.BlockSpec(memory_space=pltpu.SMEM),
            pl.BlockSpec((None, None, tq, wide), lambda bi, h, i: (0, bi, i, h)),
            pl.BlockSpec((None, None, t, wide), lambda bi, h, i: (1, bi, 0, h)),
            pl.BlockSpec((None, None, t, wide), lambda bi, h, i: (2, bi, 0, h)),
        ],
        out_specs=pl.BlockSpec((None, tq, wide), lambda bi, h, i: (bi, i, h)),
        out_shape=jax.ShapeDtypeStruct((b, t, width), F32),
        compiler_params=_params("parallel", "parallel", "arbitrary"),
        name="attn_prompt",
    )(sb_bias, qkv16, qkv16, qkv16)


def _sattn_kernel(pt_ref, qbd_ref, bias_ref, kn_ref, vn_ref, *rest,
                  scale, heads, tsteps, ppb):
    del pt_ref
    kc_refs, vc_refs = rest[:ppb], rest[ppb:2 * ppb]
    o_ref, acc_ref, carry_ref = rest[2 * ppb:]
    j = pl.program_id(1)
    rows = carry_ref.shape[0]
    page = kn_ref.shape[0]
    qbd = qbd_ref[...]

    def process(kb, vb, bias, causal):
        z = _nt_dot(qbd, kb) * (scale * LOG2E) + bias
        sub = min(z.shape[1], MXU_DIM)
        log_beta, sp0, stack = _sb_split(z, sub, causal)
        after = jnp.dot(stack, _suffix_matrix(sub), preferred_element_type=F32)
        w, carry = _sb_finish(log_beta, sp0, after, carry_ref[:, :1], causal)
        acc_ref[...] += jnp.dot(w, vb, preferred_element_type=F32)
        carry_ref[...] = jnp.broadcast_to(carry, carry_ref.shape)

    @pl.when(j == 0)
    def _():
        acc_ref[...] = jnp.zeros_like(acc_ref)
        carry_ref[...] = jnp.zeros_like(carry_ref)
        row = lax.broadcasted_iota(jnp.int32, (rows, page), 0)
        col = lax.broadcasted_iota(jnp.int32, (rows, page), 1)
        process(kn_ref[...].astype(BF16), vn_ref[...].astype(BF16), bias_ref[:, :page],
                col < (row % tsteps))

    def gather(ref):
        return jnp.concatenate(
            [ref[pl.ds(h, page, stride=heads), :].astype(BF16) for h in range(heads)], axis=1)

    kb = jnp.concatenate([gather(r) for r in kc_refs], axis=0)
    vb = jnp.concatenate([gather(r) for r in vc_refs], axis=0)
    process(kb, vb, bias_ref[...], None)

    @pl.when(j == pl.num_programs(1) - 1)
    def _():
        for h in range(heads):
            o_ref[:, h * HEAD_DIM:(h + 1) * HEAD_DIM] = (
                acc_ref[h * tsteps:(h + 1) * tsteps, h * HEAD_DIM:(h + 1) * HEAD_DIM])


def _attn_sample(q, k_new, v_new, cache_k, cache_v, layer, page_table, sb_bias, *, ppb):
    b, ts, width = q.shape
    heads = width // HEAD_DIM
    page = cache_k.shape[2] // heads
    n_pages = page_table.shape[1]
    ppb = min(ppb, n_pages)
    rows = heads * ts
    q = q.reshape(b, ts, heads, HEAD_DIM)
    eye = jnp.eye(heads, dtype=F32)
    qbd = (q.transpose(0, 2, 1, 3)[:, :, :, None, :] * eye[None, :, None, :, None])
    qbd = qbd.reshape(b, rows, width).astype(BF16)
    bias = jnp.broadcast_to(jnp.repeat(sb_bias * LOG2E, ts)[:, None], (rows, ppb * page))
    pad = ((0, 0), (0, page - ts), (0, 0))
    k_new = jnp.pad(k_new, pad)
    v_new = jnp.pad(v_new, pad)
    kern = functools.partial(_sattn_kernel, scale=HEAD_DIM ** -0.5, heads=heads, tsteps=ts,
                             ppb=ppb)

    def page_spec(r):
        return pl.BlockSpec(
            (None, None, page * heads, HEAD_DIM),
            lambda bi, j, pt: (layer, pt[bi, n_pages - ppb * (j + 1) + r], 0, 0))

    return pl.pallas_call(
        kern,
        grid_spec=pltpu.PrefetchScalarGridSpec(
            num_scalar_prefetch=1,
            grid=(b, n_pages // ppb),
            in_specs=[
                pl.BlockSpec((None, rows, width), lambda bi, j, pt: (bi, 0, 0)),
                pl.BlockSpec((rows, ppb * page), lambda bi, j, pt: (0, 0)),
                pl.BlockSpec((None, page, width), lambda bi, j, pt: (bi, 0, 0)),
                pl.BlockSpec((None, page, width), lambda bi, j, pt: (bi, 0, 0)),
            ] + [page_spec(r) for r in range(ppb)] * 2,
            out_specs=pl.BlockSpec((None, ts, width), lambda bi, j, pt: (bi, 0, 0)),
            scratch_shapes=[pltpu.VMEM((rows, width), F32), pltpu.VMEM((rows, HEAD_DIM), F32)],
        ),
        out_shape=jax.ShapeDtypeStruct((b, ts, width), F32),
        compiler_params=_params("parallel", "arbitrary"),
        name="attn_sample",
    )(page_table, qbd, bias, k_new, v_new, *([cache_k] * ppb), *([cache_v] * ppb))


def _cmul_add(xr, xi, ar, ai, br, bi):
    return xr + (ar * br - ai * bi), xi + (ar * bi + ai * br)


def _ssm_kernel(u_ref, h0_ref, bbd_ref, cbd_ref, d_ref, lam_ref, tbl_ref, y_ref, hl_ref,
                pad_ref, up_ref, s_ref, carry_ref):
    t = pl.program_id(2)
    tc, lanes = s_ref.shape
    half = lanes // 2
    seg = tc // SUBLANES
    unroll = min(seg, 8)
    rows8 = lambda k: pl.ds(pl.multiple_of(k * SUBLANES, SUBLANES), SUBLANES)

    @pl.when(t == 0)
    def _():
        carry_ref[...] = jnp.broadcast_to(h0_ref[...], carry_ref.shape)

    if seg > 1:
        pitch = seg + SUBLANES
        for j in range(SUBLANES):
            pad_ref[j * pitch:j * pitch + seg, :] = u_ref[j * seg:(j + 1) * seg, :]

        def permute(k, _):
            up_ref[rows8(k), :] = pad_ref[pl.ds(k, SUBLANES, stride=pitch), :]
            return _
        lax.fori_loop(0, seg, permute, 0, unroll=unroll)
        u = up_ref[...]
    else:
        u = u_ref[...]
    s_ref[...] = jnp.dot(u.astype(BF16), bbd_ref[...], preferred_element_type=F32)

    lam_r = jnp.broadcast_to(lam_ref[:, :half], (SUBLANES, half))
    lam_i = jnp.broadcast_to(lam_ref[:, half:], (SUBLANES, half))

    def local(k, c):
        return _cmul_add(s_ref[rows8(k), :half], s_ref[rows8(k), half:], lam_r, lam_i, *c)

    zero = jnp.zeros((SUBLANES, half), F32)
    xr, xi = lax.fori_loop(0, seg, local, (zero, zero), unroll=unroll)

    for n, d in enumerate((1, 2, 4)):
        xr, xi = _cmul_add(xr, xi, tbl_ref[2 * n], tbl_ref[2 * n + 1],
                           pltpu.roll(xr, d, 0), pltpu.roll(xi, d, 0))
    cr = carry_ref[:, :half]
    ci = carry_ref[:, half:]
    fr, fi = _cmul_add(xr, xi, tbl_ref[6], tbl_ref[7], cr, ci)
    first = lax.broadcasted_iota(jnp.int32, (SUBLANES, half), 0) == 0
    sr = jnp.where(first, cr, pltpu.roll(fr, 1, 0))
    si = jnp.where(first, ci, pltpu.roll(fi, 1, 0))
    last = SUBLANES - 1
    carry_ref[:, :half] = jnp.broadcast_to(fr[last:, :], (SUBLANES, half))
    carry_ref[:, half:] = jnp.broadcast_to(fi[last:, :], (SUBLANES, half))

    def true_scan(k, c):
        hr, hi = _cmul_add(s_ref[rows8(k), :half], s_ref[rows8(k), half:], lam_r, lam_i, *c)
        s_ref[rows8(k), :half] = hr
        s_ref[rows8(k), half:] = hi
        return hr, hi

    lax.fori_loop(0, seg, true_scan, (sr, si), unroll=unroll)

    y = (jnp.dot(s_ref[...].astype(BF16), cbd_ref[...], preferred_element_type=F32)
         + d_ref[...] * u)
    if seg > 1:
        up_ref[...] = y

        def unpermute(k, _):
            y_ref[pl.ds(k, SUBLANES, stride=seg), :] = up_ref[rows8(k), :]
            return _
        lax.fori_loop(0, seg, unpermute, 0, unroll=unroll)
    else:
        y_ref[...] = y

    @pl.when(t == pl.num_programs(2) - 1)
    def _():
        hl_ref[...] = carry_ref[:1, :]


def _ssm_params(lam_re, lam_im, log_dt, b_re, b_im, c_re, c_im, d_skip):
    g, p = lam_re.shape
    c = b_re.shape[-1]
    nb = g // GROUPS_PER_BLOCK
    gb = GROUPS_PER_BLOCK
    lam = lax.complex(lam_re, lam_im)
    dt = jnp.exp(log_dt)[:, None]
    lam_bar = jnp.exp(lam * dt)
    b_bar = ((lam_bar - 1.0) / lam)[..., None] * lax.complex(b_re, b_im)
    eye = jnp.eye(gb, dtype=F32)

    def blockdiag_in(x):
        x = x.reshape(nb, gb, p, c).transpose(0, 1, 3, 2)
        x = x[:, :, :, None, :] * eye[None, :, None, :, None]
        return x.reshape(nb, gb * c, gb * p)

    bbd = jnp.concatenate([blockdiag_in(b_bar.real), blockdiag_in(b_bar.imag)], axis=-1)

    def blockdiag_out(x):
        x = x.reshape(nb, gb, c, p).transpose(0, 1, 3, 2)
        x = x[:, :, :, None, :] * eye[None, :, None, :, None]
        return x.reshape(nb, gb * p, gb * c)

    cbd = jnp.concatenate([blockdiag_out(c_re), -blockdiag_out(c_im)], axis=1)

    lam_blk = lam_bar.reshape(nb, 1, gb * p)
    lam_blk = jnp.concatenate([lam_blk.real, lam_blk.imag], axis=-1)
    dsk = d_skip.reshape(nb, 1, gb * c)
    return (bbd.astype(BF16), cbd.astype(BF16), dsk, lam_blk), lam_bar


def _segment_table(lam_bar, seg):
    g, p = lam_bar.shape
    nb = g // GROUPS_PER_BLOCK
    assert seg & (seg - 1) == 0
    base = lam_bar
    for _ in range(seg.bit_length() - 1):
        base = base * base
    pw = [base]
    for _ in range(SUBLANES - 1):
        pw.append(pw[-1] * base)
    pw = jnp.stack(pw)
    rows = jnp.arange(SUBLANES)[:, None, None]
    tbls = []
    for d in (1, 2, 4):
        m = jnp.where(rows >= d, jnp.broadcast_to(pw[d - 1], pw.shape), 0.0)
        tbls += [m.real, m.imag]
    tbls += [pw.real, pw.imag]
    tbl = jnp.stack(tbls)
    return tbl.reshape(8, SUBLANES, nb, GROUPS_PER_BLOCK * p).transpose(2, 0, 1, 3)


def _ssm(qu, h0, params, tbl, layer, *, tc):
    bbd, cbd, dsk, lam = params
    _, b, t, width = qu.shape
    _, nb, uw, sw = bbd.shape
    assert t % tc == 0 and tc % SUBLANES == 0
    seg = tc // SUBLANES
    return pl.pallas_call(
        _ssm_kernel,
        grid=(b, nb, t // tc),
        in_specs=[
            pl.BlockSpec((None, None, tc, uw), lambda bi, g, ti: (1, bi, ti, g)),
            pl.BlockSpec((None, None, 1, sw), lambda bi, g, ti: (bi, g, 0, 0)),
            pl.BlockSpec((None, None, uw, sw), lambda bi, g, ti: (layer, g, 0, 0)),
            pl.BlockSpec((None, None, sw, uw), lambda bi, g, ti: (layer, g, 0, 0)),
            pl.BlockSpec((None, None, 1, uw), lambda bi, g, ti: (layer, g, 0, 0)),
            pl.BlockSpec((None, None, 1, sw), lambda bi, g, ti: (layer, g, 0, 0)),
            pl.BlockSpec((None, None, 8, SUBLANES, sw // 2),
                         lambda bi, g, ti: (layer, g, 0, 0, 0)),
        ],
        out_specs=[
            pl.BlockSpec((None, tc, uw), lambda bi, g, ti: (bi, ti, g)),
            pl.BlockSpec((None, None, 1, sw), lambda bi, g, ti: (bi, g, 0, 0)),
        ],
        out_shape=[jax.ShapeDtypeStruct((b, t, width), F32),
                   jax.ShapeDtypeStruct((b, nb, 1, sw), F32)],
        scratch_shapes=[pltpu.VMEM((SUBLANES * (seg + SUBLANES), uw), F32),
                        pltpu.VMEM((tc, uw), F32), pltpu.VMEM((tc, sw), F32),
                        pltpu.VMEM((SUBLANES, sw), F32)],
        compiler_params=_params("parallel", "parallel", "arbitrary"),
        name="s5_scan",
    )(qu, h0, bbd, cbd, dsk, lam, tbl)


def _pack_state(re, im):
    b, g, p = re.shape
    nb = g // GROUPS_PER_BLOCK
    re = re.reshape(b, nb, 1, GROUPS_PER_BLOCK * p)
    im = im.reshape(b, nb, 1, GROUPS_PER_BLOCK * p)
    return jnp.concatenate([re, im], axis=-1)


def _unpack_state(hl, g, p):
    b = hl.shape[0]
    half = hl.shape[-1] // 2
    return hl[..., :half].reshape(b, g, p), hl[..., half:].reshape(b, g, p)


def _mix_kernel(h_ref, a_ref, y_ref, wglu_ref, gattn_ref, gssm_ref, wout_ref, gpost_ref, o_ref):
    y = y_ref[...]
    g = 0.5 * y * (1.0 + jnp.tanh(math.sqrt(2.0 / math.pi) * (y + 0.044715 * (y * y * y))))
    gate = jnp.dot(g.astype(BF16), wglu_ref[...], preferred_element_type=F32)
    ssm_out = g * jax.nn.sigmoid(gate)
    na = _rms(a_ref[...], gattn_ref[...]).astype(BF16)
    ns = _rms(ssm_out, gssm_ref[...]).astype(BF16)
    aw = na.shape[1]
    m = (jnp.dot(na, wout_ref[:aw, :], preferred_element_type=F32)
         + jnp.dot(ns, wout_ref[aw:, :], preferred_element_type=F32))
    o_ref[...] = h_ref[...] + _rms(m, gpost_ref[...])


def _mix(h, attn, y, w_glu, g_attn, g_ssm, w_out, g_post, layer, *, tm):
    m, d = h.shape
    aw = attn.shape[1]
    sw = y.shape[1]
    tm = min(tm, m)
    return pl.pallas_call(
        _mix_kernel,
        grid=(m // tm,),
        in_specs=[
            pl.BlockSpec((tm, d), lambda i: (i, 0)),
            pl.BlockSpec((tm, aw), lambda i: (i, 0)),
            pl.BlockSpec((tm, sw), lambda i: (i, 0)),
            pl.BlockSpec((None, sw, sw), lambda i: (layer, 0, 0)),
            pl.BlockSpec((None, 1, aw), lambda i: (layer, 0, 0)),
            pl.BlockSpec((None, 1, sw), lambda i: (layer, 0, 0)),
            pl.BlockSpec((None, d, d), lambda i: (layer, 0, 0)),
            pl.BlockSpec((None, 1, d), lambda i: (layer, 0, 0)),
        ],
        out_specs=pl.BlockSpec((tm, d), lambda i: (i, 0)),
        out_shape=jax.ShapeDtypeStruct((m, d), F32),
        compiler_params=_params("parallel"),
        name="mix_out",
    )(h, attn, y, w_glu, g_attn, g_ssm, w_out, g_post)


FFN_TM = 1024
FFN_TF = 256
PROJ_TM = 512
MIX_TM = 512
ATTN_TQ = 256
ATTN_HEADS_PER_STEP = 8
PAGES_PER_STEP = 16
SSM_TC = 1024


def kernel(x_prompt, x_sample, cache_k, cache_v, state_ssm_re, state_ssm_im, page_table, g_ffn1_pre, w_ffn1_gate, w_ffn1_up, w_ffn1_down, g_ffn1_post, g_mix_pre, w_in, sb_bias, lambda_re, lambda_im, log_dt, b_re, b_im, c_re, c_im, d_skip, w_glu, g_attn_out, g_ssm_out, w_out, g_mix_post, g_ffn2_pre, w_ffn2_gate, w_ffn2_up, w_ffn2_down, g_ffn2_post):
    depth = w_in.shape[0]
    bp, tp, d = x_prompt.shape
    bs, ts, _ = x_sample.shape
    groups, states = lambda_re.shape[1:]
    pool, page, heads = cache_k.shape[1:4]
    width = heads * HEAD_DIM
    ck = cache_k.reshape(depth, pool, page * heads, HEAD_DIM)
    cv = cache_v.reshape(depth, pool, page * heads, HEAD_DIM)

    hp = x_prompt.reshape(bp * tp, d)
    hs = x_sample.reshape(bs * ts, d)
    kp = jnp.zeros((depth, bp * tp, width), F32)
    vp = jnp.zeros((depth, bp * tp, width), F32)
    ks = jnp.zeros((depth, bs * ts, width), F32)
    vs = jnp.zeros((depth, bs * ts, width), F32)
    zero_state = jnp.zeros((bp, groups // GROUPS_PER_BLOCK, 1, 2 * GROUPS_PER_BLOCK * states), F32)
    states_out = [[] for _ in range(4)]
    tc_p = min(SSM_TC, tp)
    tc_s = min(SSM_TC, ts)

    gains = lambda g: g.reshape(depth, 1, -1)
    ffn1 = (gains(g_ffn1_pre), w_ffn1_gate, w_ffn1_up, w_ffn1_down, gains(g_ffn1_post))
    ffn2 = (gains(g_ffn2_pre), w_ffn2_gate, w_ffn2_up, w_ffn2_down, gains(g_ffn2_post))
    g_in = gains(g_mix_pre)
    w_in_b = w_in.astype(BF16)
    mixw = (w_glu.astype(BF16), gains(g_attn_out), gains(g_ssm_out), w_out.astype(BF16),
            gains(g_mix_post))
    s5, lam_bar = jax.vmap(_ssm_params)(lambda_re, lambda_im, log_dt, b_re, b_im, c_re, c_im,
                                        d_skip)
    tbl_p = jax.vmap(functools.partial(_segment_table, seg=tc_p // SUBLANES))(lam_bar)
    tbl_s = jax.vmap(functools.partial(_segment_table, seg=tc_s // SUBLANES))(lam_bar)

    for l in range(depth):

        hp = _ffn(hp, *ffn1, l, tm=FFN_TM, tf=FFN_TF)
        hs = _ffn(hs, *ffn1, l, tm=FFN_TM, tf=FFN_TF)

        qup, kp, vp, qkv16 = _inproj(hp, g_in, w_in_b, kp, vp, l, tm=PROJ_TM)
        qus, ks, vs, _ = _inproj(hs, g_in, w_in_b, ks, vs, l, tm=PROJ_TM)
        qup = qup.reshape(2, bp, tp, width)
        qus = qus.reshape(2, bs, ts, width)

        attn_p = _attn_prompt(qkv16.reshape(3, bp, tp, width), sb_bias[l],
                              tq=ATTN_TQ, hps=ATTN_HEADS_PER_STEP)
        attn_s = _attn_sample(qus[0], ks[l].reshape(bs, ts, width), vs[l].reshape(bs, ts, width),
                              ck, cv, l, page_table, sb_bias[l], ppb=PAGES_PER_STEP)

        yp, hlp = _ssm(qup, zero_state, s5, tbl_p, l, tc=tc_p)
        ys, hls = _ssm(qus, _pack_state(state_ssm_re[l], state_ssm_im[l]), s5, tbl_s, l, tc=tc_s)

        hp = _mix(hp, attn_p.reshape(bp * tp, width), yp.reshape(bp * tp, -1), *mixw, l, tm=MIX_TM)
        hs = _mix(hs, attn_s.reshape(bs * ts, width), ys.reshape(bs * ts, -1), *mixw, l, tm=MIX_TM)

        hp = _ffn(hp, *ffn2, l, tm=FFN_TM, tf=FFN_TF)
        hs = _ffn(hs, *ffn2, l, tm=FFN_TM, tf=FFN_TF)

        for dst, val in zip(states_out, _unpack_state(hlp, groups, states)
                            + _unpack_state(hls, groups, states)):
            dst.append(val)

    return (hp.reshape(bp, tp, d), hs.reshape(bs, ts, d),
            kp.reshape(depth, bp, tp, heads, HEAD_DIM), vp.reshape(depth, bp, tp, heads, HEAD_DIM),
            ks.reshape(depth, bs, ts, heads, HEAD_DIM), vs.reshape(depth, bs, ts, heads, HEAD_DIM),
            ) + tuple(jnp.stack(o) for o in states_out)
```

```python
import functools
import math

import jax
import jax.numpy as jnp
from jax import lax
from jax.experimental import pallas as pl
from jax.experimental.pallas import tpu as pltpu

F32 = jnp.float32
BF16 = jnp.bfloat16

RMS_EPS = 1e-6
HEAD_DIM = 128
MXU_DIM = 256
SSM_GROUP = 16
SSM_STATE = 64
GROUPS_PER_BLOCK = 8
SUBLANES = 8
VMEM_LIMIT = 62 * 1024 * 1024


def _params(*sem):
    return pltpu.CompilerParams(dimension_semantics=sem, vmem_limit_bytes=VMEM_LIMIT)


def _rms(x, g):
    return x * lax.rsqrt(jnp.mean(x * x, axis=-1, keepdims=True) + RMS_EPS) * g


LOG2E = math.log2(math.e)


def _softplus2(z2):
    return jnp.maximum(z2, 0.0) + jnp.log2(1.0 + jnp.exp2(-jnp.abs(z2)))


def _nt_dot(a, b):
    return lax.dot_general(a, b, (((1,), (1,)), ((), ())), preferred_element_type=F32)


def _suffix_matrix(s):
    r = lax.broadcasted_iota(jnp.int32, (2 * s, s), 0)
    c = lax.broadcasted_iota(jnp.int32, (2 * s, s), 1)
    return jnp.where((r > c) & ((r < s) | (r > c + s)), 1.0, 0.0).astype(BF16)


def _sb_split(z, s, causal=None):
    m, w = z.shape
    n = w // s
    sp = _softplus2(z)
    log_beta = z - sp
    if causal is not None:
        sp = jnp.where(causal, sp, 0.0)
    if n > 1:
        sp = jnp.concatenate([sp[:, i * s:(i + 1) * s] for i in range(n)], axis=0)
    hi = sp.astype(BF16)
    lo = (sp - hi.astype(F32)).astype(BF16)
    return log_beta, sp[:, :1], jnp.concatenate([hi, lo], axis=1)


def _sb_finish(log_beta, sp0, after, carry, causal=None):
    m, w = log_beta.shape
    n = after.shape[0] // m
    total = after[:, :1] + sp0
    parts = [None] * n
    for i in reversed(range(n)):
        parts[i] = after[i * m:(i + 1) * m] + carry
        carry = carry + total[i * m:(i + 1) * m]
    after = parts[0] if n == 1 else jnp.concatenate(parts, axis=1)
    wgt = jnp.exp2(log_beta - after)
    if causal is not None:
        wgt = jnp.where(causal, wgt, 0.0)
    return wgt.astype(BF16), carry


def _ffn_kernel(x_ref, xs_ref, gpre_ref, wg_ref, wu_ref, wd_ref, gpost_ref, o_ref, os_ref, n_ref):
    i = pl.program_id(0)
    f = pl.program_id(1)
    last = pl.num_programs(1) - 1
    tm = x_ref.shape[0]

    @pl.when(f == 0)
    def _():
        n_ref[:tm, :] = _rms(x_ref[...], gpre_ref[...]).astype(BF16)
        o_ref[...] = jnp.zeros_like(o_ref)

    @pl.when((f == 0) & (i == 0))
    def _():
        n_ref[tm:, :] = _rms(xs_ref[...], gpre_ref[...]).astype(BF16)
        os_ref[...] = jnp.zeros_like(os_ref)

    def swiglu(n):
        gate = jnp.dot(n, wg_ref[...].astype(BF16), preferred_element_type=F32)
        up = jnp.dot(n, wu_ref[...].astype(BF16), preferred_element_type=F32)
        act = gate * jax.nn.sigmoid(gate) * up
        return jnp.dot(act.astype(BF16), wd_ref[...].astype(BF16), preferred_element_type=F32)

    @pl.when(i == 0)
    def _():
        res = swiglu(n_ref[...])
        o_ref[...] += res[:tm]
        os_ref[...] += res[tm:]

    @pl.when(i > 0)
    def _():
        o_ref[...] += swiglu(n_ref[:tm, :])

    @pl.when(f == last)
    def _():
        o_ref[...] = x_ref[...] + 0.5 * _rms(o_ref[...], gpost_ref[...])

    @pl.when((f == last) & (i == 0))
    def _():
        os_ref[...] = xs_ref[...] + 0.5 * _rms(os_ref[...], gpost_ref[...])


def _ffn(x, xs, g_pre, w_gate, w_up, w_down, g_post, layer, *, tm, tf):
    m, d = x.shape
    ms = xs.shape[0]
    dff = w_gate.shape[2]
    tm = min(tm, m)
    return pl.pallas_call(
        _ffn_kernel,
        grid=(m // tm, dff // tf),
        in_specs=[
            pl.BlockSpec((tm, d), lambda i, f: (i, 0)),
            pl.BlockSpec((ms, d), lambda i, f: (0, 0)),
            pl.BlockSpec((None, 1, d), lambda i, f: (layer, 0, 0)),
            pl.BlockSpec((None, d, tf), lambda i, f: (layer, 0, f)),
            pl.BlockSpec((None, d, tf), lambda i, f: (layer, 0, f)),
            pl.BlockSpec((None, tf, d), lambda i, f: (layer, f, 0)),
            pl.BlockSpec((None, 1, d), lambda i, f: (layer, 0, 0)),
        ],
        out_specs=[pl.BlockSpec((tm, d), lambda i, f: (i, 0)),
                   pl.BlockSpec((ms, d), lambda i, f: (0, 0))],
        out_shape=[jax.ShapeDtypeStruct((m, d), F32), jax.ShapeDtypeStruct((ms, d), F32)],
        scratch_shapes=[pltpu.VMEM((tm + ms, d), BF16)],
        compiler_params=_params("arbitrary", "arbitrary"),
        name="ffn_half",
    )(x, xs, g_pre, w_gate, w_up, w_down, g_post)


def _inproj_kernel(x_ref, g_ref, w_ref, kin_ref, vin_ref, qu_ref, k_ref, v_ref, qkv16_ref, n_ref,
                   *, q_scale):
    del kin_ref, vin_ref
    j = pl.program_id(1)

    @pl.when(j == 0)
    def _():
        n_ref[...] = _rms(x_ref[...], g_ref[...]).astype(BF16)

    res = jnp.dot(n_ref[...], w_ref[...], preferred_element_type=F32)

    @pl.when((j == 0) | (j == 3))
    def _():
        qu_ref[...] = res

    @pl.when(j == 0)
    def _():
        qkv16_ref[...] = (res * q_scale).astype(BF16)

    @pl.when(j == 1)
    def _():
        k_ref[...] = res
        qkv16_ref[...] = res.astype(BF16)

    @pl.when(j == 2)
    def _():
        v_ref[...] = res
        qkv16_ref[...] = res.astype(BF16)


def _inproj(x, g, w_in, k_all, v_all, layer, *, tm):
    m, d = x.shape
    width = w_in.shape[2] // 4
    tm = min(tm, m)
    return pl.pallas_call(
        functools.partial(_inproj_kernel, q_scale=HEAD_DIM ** -0.5 * LOG2E),
        grid=(m // tm, 4),
        in_specs=[
            pl.BlockSpec((tm, d), lambda i, j: (i, 0)),
            pl.BlockSpec((None, 1, d), lambda i, j: (layer, 0, 0)),
            pl.BlockSpec((None, d, width), lambda i, j: (layer, 0, j)),
            pl.BlockSpec(memory_space=pl.ANY),
            pl.BlockSpec(memory_space=pl.ANY),
        ],
        out_specs=[
            pl.BlockSpec((None, tm, width), lambda i, j: (j // 3, i, 0)),
            pl.BlockSpec((None, tm, width), lambda i, j: (layer, i, 0)),
            pl.BlockSpec((None, tm, width), lambda i, j: (layer, i, 0)),
            pl.BlockSpec((None, tm, width), lambda i, j: (jnp.minimum(j, 2), i, 0)),
        ],
        out_shape=[jax.ShapeDtypeStruct((2, m, width), F32),
                   jax.ShapeDtypeStruct(k_all.shape, F32),
                   jax.ShapeDtypeStruct(v_all.shape, F32),
                   jax.ShapeDtypeStruct((3, m, width), BF16)],
        input_output_aliases={3: 1, 4: 2},
        scratch_shapes=[pltpu.VMEM((tm, d), BF16)],
        compiler_params=_params("parallel", "arbitrary"),
        name="in_proj",
    )(x, g, w_in, k_all, v_all)


def _attn_kernel(bias_ref, q_ref, k_ref, v_ref, o_ref, *, tq, hps):
    hb = pl.program_id(1)
    i = pl.program_id(2)
    sub = min(tq, MXU_DIM)
    suffix = _suffix_matrix(sub)
    row = lax.broadcasted_iota(jnp.int32, (tq, tq), 0)
    col = lax.broadcasted_iota(jnp.int32, (tq, tq), 1)
    lanes = [slice(n * HEAD_DIM, (n + 1) * HEAD_DIM) for n in range(hps)]
    biases = [bias_ref[hb * hps + n] * LOG2E for n in range(hps)]

    def step(j, state, masked):
        keys = pl.ds(pl.multiple_of(j * tq, tq), tq)
        causal = (col < row) if masked else None
        zs = [_nt_dot(q_ref[:, ln], k_ref[keys, ln]) + biases[n] for n, ln in enumerate(lanes)]
        split = [_sb_split(z, sub, causal) for z in zs]
        afters = [jnp.dot(st, suffix, preferred_element_type=F32) for _, _, st in split]
        new = []
        for n, ln in enumerate(lanes):
            carry, acc = state[n]
            w, carry = _sb_finish(split[n][0], split[n][1], afters[n], carry, causal)
            new.append((carry, acc + jnp.dot(w, v_ref[keys, ln], preferred_element_type=F32)))
        return tuple(new)

    state = tuple((jnp.zeros((tq, 1), F32), jnp.zeros((tq, HEAD_DIM), F32)) for _ in lanes)
    state = step(i, state, True)
    state = lax.fori_loop(0, i, lambda t, s: step(i - 1 - t, s, False), state)
    for n, ln in enumerate(lanes):
        o_ref[:, ln] = state[n][1]


def _attn_prompt(qkv16, sb_bias, *, tq, hps):
    _, b, t, width = qkv16.shape
    heads = width // HEAD_DIM
    tq = min(tq, t)
    hps = min(hps, heads)
    wide = hps * HEAD_DIM
    return pl.pallas_call(
        functools.partial(_attn_kernel, tq=tq, hps=hps),
        grid=(b, heads // hps, t // tq),
        in_specs=[
            pl.BlockSpec(memory_space=pltpu.SMEM),
            pl.BlockSpec((None, None, tq, wide), lambda bi, h, i: (0, bi, i, h)),
            pl.BlockSpec((None, None, t, wide), lambda bi, h, i: (1, bi, 0, h)),
            pl.BlockSpec((None, None, t, wide), lambda bi, h, i: (2, bi, 0, h)),
        ],
        out_specs=pl.BlockSpec((None, tq, wide), lambda bi, h, i: (bi, i, h)),
        out_shape=jax.ShapeDtypeStruct((b, t, width), F32),
        compiler_params=_params("parallel", "parallel", "arbitrary"),
        name="attn_prompt",
    )(sb_bias, qkv16, qkv16, qkv16)


def _sattn_kernel(pt_ref, qbd_ref, bias_ref, kn_ref, vn_ref, *rest,
                  scale, heads, tsteps, ppb):
    del pt_ref
    kc_refs, vc_refs = rest[:ppb], rest[ppb:2 * ppb]
    o_ref, acc_ref, carry_ref = rest[2 * ppb:]
    j = pl.program_id(1)
    rows = carry_ref.shape[0]
    page = kn_ref.shape[0]
    qbd = qbd_ref[...]

    def process(kb, vb, bias, causal):
        z = _nt_dot(qbd, kb) * (scale * LOG2E) + bias
        sub = min(z.shape[1], MXU_DIM)
        log_beta, sp0, stack = _sb_split(z, sub, causal)
        after = jnp.dot(stack, _suffix_matrix(sub), preferred_element_type=F32)
        w, carry = _sb_finish(log_beta, sp0, after, carry_ref[:, :1], causal)
        acc_ref[...] += jnp.dot(w, vb, preferred_element_type=F32)
        carry_ref[...] = jnp.broadcast_to(carry, carry_ref.shape)

    @pl.when(j == 0)
    def _():
        acc_ref[...] = jnp.zeros_like(acc_ref)
        carry_ref[...] = jnp.zeros_like(carry_ref)
        row = lax.broadcasted_iota(jnp.int32, (rows, page), 0)
        col = lax.broadcasted_iota(jnp.int32, (rows, page), 1)
        process(kn_ref[...].astype(BF16), vn_ref[...].astype(BF16), bias_ref[:, :page],
                col < (row % tsteps))

    def gather(ref):
        return jnp.concatenate(
            [ref[pl.ds(h, page, stride=heads), :].astype(BF16) for h in range(heads)], axis=1)

    kb = jnp.concatenate([gather(r) for r in kc_refs], axis=0)
    vb = jnp.concatenate([gather(r) for r in vc_refs], axis=0)
    process(kb, vb, bias_ref[...], None)

    @pl.when(j == pl.num_programs(1) - 1)
    def _():
        for h in range(heads):
            o_ref[:, h * HEAD_DIM:(h + 1) * HEAD_DIM] = (
                acc_ref[h * tsteps:(h + 1) * tsteps, h * HEAD_DIM:(h + 1) * HEAD_DIM])


def _attn_sample(q, k_new, v_new, cache_k, cache_v, layer, page_table, sb_bias, *, ppb):
    b, ts, width = q.shape
    heads = width // HEAD_DIM
    page = cache_k.shape[2] // heads
    n_pages = page_table.shape[1]
    ppb = min(ppb, n_pages)
    rows = heads * ts
    q = q.reshape(b, ts, heads, HEAD_DIM)
    eye = jnp.eye(heads, dtype=F32)
    qbd = (q.transpose(0, 2, 1, 3)[:, :, :, None, :] * eye[None, :, None, :, None])
    qbd = qbd.reshape(b, rows, width).astype(BF16)
    bias = jnp.broadcast_to(jnp.repeat(sb_bias * LOG2E, ts)[:, None], (rows, ppb * page))
    pad = ((0, 0), (0, page - ts), (0, 0))
    k_new = jnp.pad(k_new, pad)
    v_new = jnp.pad(v_new, pad)
    kern = functools.partial(_sattn_kernel, scale=HEAD_DIM ** -0.5, heads=heads, tsteps=ts,
                             ppb=ppb)

    def page_spec(r):
        return pl.BlockSpec(
            (None, None, page * heads, HEAD_DIM),
            lambda bi, j, pt: (layer, pt[bi, n_pages - ppb * (j + 1) + r], 0, 0))

    return pl.pallas_call(
        kern,
        grid_spec=pltpu.PrefetchScalarGridSpec(
            num_scalar_prefetch=1,
            grid=(b, n_pages // ppb),
            in_specs=[
                pl.BlockSpec((None, rows, width), lambda bi, j, pt: (bi, 0, 0)),
                pl.BlockSpec((rows, ppb * page), lambda bi, j, pt: (0, 0)),
                pl.BlockSpec((None, page, width), lambda bi, j, pt: (bi, 0, 0)),
                pl.BlockSpec((None, page, width), lambda bi, j, pt: (bi, 0, 0)),
            ] + [page_spec(r) for r in range(ppb)] * 2,
            out_specs=pl.BlockSpec((None, ts, width), lambda bi, j, pt: (bi, 0, 0)),
            scratch_shapes=[pltpu.VMEM((rows, width), F32), pltpu.VMEM((rows, HEAD_DIM), F32)],
        ),
        out_shape=jax.ShapeDtypeStruct((b, ts, width), F32),
        compiler_params=_params("parallel", "arbitrary"),
        name="attn_sample",
    )(page_table, qbd, bias, k_new, v_new, *([cache_k] * ppb), *([cache_v] * ppb))


def _cmul_add(xr, xi, ar, ai, br, bi):
    return xr + (ar * br - ai * bi), xi + (ar * bi + ai * br)


def _ssm_kernel(u_ref, h0_ref, bbd_ref, cbd_ref, d_ref, lam_ref, tbl_ref, y_ref, hl_ref,
                pad_ref, up_ref, s_ref, carry_ref):
    for b in range(u_ref.shape[0]):
        _ssm_chunk(u_ref.at[b], h0_ref.at[b], bbd_ref, cbd_ref, d_ref, lam_ref, tbl_ref,
                   y_ref.at[b], hl_ref.at[b], pad_ref, up_ref, s_ref, carry_ref.at[b])


def _ssm_chunk(u_ref, h0_ref, bbd_ref, cbd_ref, d_ref, lam_ref, tbl_ref, y_ref, hl_ref,
               pad_ref, up_ref, s_ref, carry_ref):
    t = pl.program_id(2)
    tc, lanes = s_ref.shape
    half = lanes // 2
    seg = tc // SUBLANES
    unroll = min(seg, 8)
    rows8 = lambda k: pl.ds(pl.multiple_of(k * SUBLANES, SUBLANES), SUBLANES)

    @pl.when(t == 0)
    def _():
        carry_ref[...] = jnp.broadcast_to(h0_ref[...], carry_ref.shape)

    if seg > 1:
        pitch = seg + SUBLANES
        for j in range(SUBLANES):
            pad_ref[j * pitch:j * pitch + seg, :] = u_ref[j * seg:(j + 1) * seg, :]

        def permute(k, _):
            up_ref[rows8(k), :] = pad_ref[pl.ds(k, SUBLANES, stride=pitch), :]
            return _
        lax.fori_loop(0, seg, permute, 0, unroll=unroll)
        u = up_ref[...]
    else:
        u = u_ref[...]
    s_ref[...] = jnp.dot(u.astype(BF16), bbd_ref[...], preferred_element_type=F32)

    lam_r = jnp.broadcast_to(lam_ref[:, :half], (SUBLANES, half))
    lam_i = jnp.broadcast_to(lam_ref[:, half:], (SUBLANES, half))

    def local(k, c):
        return _cmul_add(s_ref[rows8(k), :half], s_ref[rows8(k), half:], lam_r, lam_i, *c)

    zero = jnp.zeros((SUBLANES, half), F32)
    xr, xi = lax.fori_loop(0, seg, local, (zero, zero), unroll=unroll)

    for n, d in enumerate((1, 2, 4)):
        xr, xi = _cmul_add(xr, xi, tbl_ref[2 * n], tbl_ref[2 * n + 1],
                           pltpu.roll(xr, d, 0), pltpu.roll(xi, d, 0))
    cr = carry_ref[:, :half]
    ci = carry_ref[:, half:]
    fr, fi = _cmul_add(xr, xi, tbl_ref[6], tbl_ref[7], cr, ci)
    first = lax.broadcasted_iota(jnp.int32, (SUBLANES, half), 0) == 0
    sr = jnp.where(first, cr, pltpu.roll(fr, 1, 0))
    si = jnp.where(first, ci, pltpu.roll(fi, 1, 0))
    last = SUBLANES - 1
    carry_ref[:, :half] = jnp.broadcast_to(fr[last:, :], (SUBLANES, half))
    carry_ref[:, half:] = jnp.broadcast_to(fi[last:, :], (SUBLANES, half))

    def true_scan(k, c):
        hr, hi = _cmul_add(s_ref[rows8(k), :half], s_ref[rows8(k), half:], lam_r, lam_i, *c)
        s_ref[rows8(k), :half] = hr
        s_ref[rows8(k), half:] = hi
        return hr, hi

    lax.fori_loop(0, seg, true_scan, (sr, si), unroll=unroll)

    y = (jnp.dot(s_ref[...].astype(BF16), cbd_ref[...], preferred_element_type=F32)
         + d_ref[...] * u)
    if seg > 1:
        up_ref[...] = y

        def unpermute(k, _):
            y_ref[pl.ds(k, SUBLANES, stride=seg), :] = up_ref[rows8(k), :]
            return _
        lax.fori_loop(0, seg, unpermute, 0, unroll=unroll)
    else:
        y_ref[...] = y

    @pl.when(t == pl.num_programs(2) - 1)
    def _():
        hl_ref[...] = carry_ref[:1, :]


def _ssm_params(lam_re, lam_im, log_dt, b_re, b_im, c_re, c_im, d_skip):
    g, p = lam_re.shape
    c = b_re.shape[-1]
    nb = g // GROUPS_PER_BLOCK
    gb = GROUPS_PER_BLOCK
    lam = lax.complex(lam_re, lam_im)
    dt = jnp.exp(log_dt)[:, None]
    lam_bar = jnp.exp(lam * dt)
    b_bar = ((lam_bar - 1.0) / lam)[..., None] * lax.complex(b_re, b_im)
    eye = jnp.eye(gb, dtype=F32)

    def blockdiag_in(x):
        x = x.reshape(nb, gb, p, c).transpose(0, 1, 3, 2)
        x = x[:, :, :, None, :] * eye[None, :, None, :, None]
        return x.reshape(nb, gb * c, gb * p)

    bbd = jnp.concatenate([blockdiag_in(b_bar.real), blockdiag_in(b_bar.imag)], axis=-1)

    def blockdiag_out(x):
        x = x.reshape(nb, gb, c, p).transpose(0, 1, 3, 2)
        x = x[:, :, :, None, :] * eye[None, :, None, :, None]
        return x.reshape(nb, gb * p, gb * c)

    cbd = jnp.concatenate([blockdiag_out(c_re), -blockdiag_out(c_im)], axis=1)

    lam_blk = lam_bar.reshape(nb, 1, gb * p)
    lam_blk = jnp.concatenate([lam_blk.real, lam_blk.imag], axis=-1)
    dsk = d_skip.reshape(nb, 1, gb * c)
    return (bbd.astype(BF16), cbd.astype(BF16), dsk, lam_blk), lam_bar


def _segment_table(lam_bar, seg):
    g, p = lam_bar.shape
    nb = g // GROUPS_PER_BLOCK
    assert seg & (seg - 1) == 0
    base = lam_bar
    for _ in range(seg.bit_length() - 1):
        base = base * base
    pw = [base]
    for _ in range(SUBLANES - 1):
        pw.append(pw[-1] * base)
    pw = jnp.stack(pw)
    rows = jnp.arange(SUBLANES)[:, None, None]
    tbls = []
    for d in (1, 2, 4):
        m = jnp.where(rows >= d, jnp.broadcast_to(pw[d - 1], pw.shape), 0.0)
        tbls += [m.real, m.imag]
    tbls += [pw.real, pw.imag]
    tbl = jnp.stack(tbls)
    return tbl.reshape(8, SUBLANES, nb, GROUPS_PER_BLOCK * p).transpose(2, 0, 1, 3)


def _ssm(qu, h0, params, tbl, layer, *, tc):
    bbd, cbd, dsk, lam = params
    _, b, t, width = qu.shape
    _, nb, uw, sw = bbd.shape
    assert t % tc == 0 and tc % SUBLANES == 0
    seg = tc // SUBLANES
    bb = b if t == tc else 1
    lead = None if bb == 1 else bb
    carry = (SUBLANES, sw) if bb == 1 else (bb, SUBLANES, sw)
    return pl.pallas_call(
        _ssm_chunk if bb == 1 else _ssm_kernel,
        grid=(b // bb, nb, t // tc),
        in_specs=[
            pl.BlockSpec((None, lead, tc, uw), lambda bi, g, ti: (1, bi, ti, g)),
            pl.BlockSpec((lead, None, 1, sw), lambda bi, g, ti: (bi, g, 0, 0)),
            pl.BlockSpec((None, None, uw, sw), lambda bi, g, ti: (layer, g, 0, 0)),
            pl.BlockSpec((None, None, sw, uw), lambda bi, g, ti: (layer, g, 0, 0)),
            pl.BlockSpec((None, None, 1, uw), lambda bi, g, ti: (layer, g, 0, 0)),
            pl.BlockSpec((None, None, 1, sw), lambda bi, g, ti: (layer, g, 0, 0)),
            pl.BlockSpec((None, None, 8, SUBLANES, sw // 2),
                         lambda bi, g, ti: (layer, g, 0, 0, 0)),
        ],
        out_specs=[
            pl.BlockSpec((lead, tc, uw), lambda bi, g, ti: (bi, ti, g)),
            pl.BlockSpec((lead, None, 1, sw), lambda bi, g, ti: (bi, g, 0, 0)),
        ],
        out_shape=[jax.ShapeDtypeStruct((b, t, width), F32),
                   jax.ShapeDtypeStruct((b, nb, 1, sw), F32)],
        scratch_shapes=[pltpu.VMEM((SUBLANES * (seg + SUBLANES), uw), F32),
                        pltpu.VMEM((tc, uw), F32), pltpu.VMEM((tc, sw), F32),
                        pltpu.VMEM(carry, F32)],
        compiler_params=_params("parallel", "parallel", "arbitrary"),
        name="s5_scan",
    )(qu, h0, bbd, cbd, dsk, lam, tbl)


def _pack_state(re, im):
    b, g, p = re.shape
    nb = g // GROUPS_PER_BLOCK
    re = re.reshape(b, nb, 1, GROUPS_PER_BLOCK * p)
    im = im.reshape(b, nb, 1, GROUPS_PER_BLOCK * p)
    return jnp.concatenate([re, im], axis=-1)


def _unpack_state(hl, g, p):
    b = hl.shape[0]
    half = hl.shape[-1] // 2
    return hl[..., :half].reshape(b, g, p), hl[..., half:].reshape(b, g, p)


def _mix_kernel(h_ref, a_ref, y_ref, wglu_ref, gattn_ref, gssm_ref, wout_ref, gpost_ref, o_ref):
    y = y_ref[...]
    g = 0.5 * y * (1.0 + jnp.tanh(math.sqrt(2.0 / math.pi) * (y + 0.044715 * (y * y * y))))
    gate = jnp.dot(g.astype(BF16), wglu_ref[...], preferred_element_type=F32)
    ssm_out = g * jax.nn.sigmoid(gate)
    na = _rms(a_ref[...], gattn_ref[...]).astype(BF16)
    ns = _rms(ssm_out, gssm_ref[...]).astype(BF16)
    aw = na.shape[1]
    m = (jnp.dot(na, wout_ref[:aw, :], preferred_element_type=F32)
         + jnp.dot(ns, wout_ref[aw:, :], preferred_element_type=F32))
    o_ref[...] = h_ref[...] + _rms(m, gpost_ref[...])


def _mix(h, attn, y, w_glu, g_attn, g_ssm, w_out, g_post, layer, *, tm):
    m, d = h.shape
    aw = attn.shape[1]
    sw = y.shape[1]
    tm = min(tm, m)
    return pl.pallas_call(
        _mix_kernel,
        grid=(m // tm,),
        in_specs=[
            pl.BlockSpec((tm, d), lambda i: (i, 0)),
            pl.BlockSpec((tm, aw), lambda i: (i, 0)),
            pl.BlockSpec((tm, sw), lambda i: (i, 0)),
            pl.BlockSpec((None, sw, sw), lambda i: (layer, 0, 0)),
            pl.BlockSpec((None, 1, aw), lambda i: (layer, 0, 0)),
            pl.BlockSpec((None, 1, sw), lambda i: (layer, 0, 0)),
            pl.BlockSpec((None, d, d), lambda i: (layer, 0, 0)),
            pl.BlockSpec((None, 1, d), lambda i: (layer, 0, 0)),
        ],
        out_specs=pl.BlockSpec((tm, d), lambda i: (i, 0)),
        out_shape=jax.ShapeDtypeStruct((m, d), F32),
        compiler_params=_params("parallel"),
        name="mix_out",
    )(h, attn, y, w_glu, g_attn, g_ssm, w_out, g_post)


FFN_TM = 1024
FFN_TF = 256
PROJ_TM = 512
MIX_TM = 512
ATTN_TQ = 256
ATTN_HEADS_PER_STEP = 8
PAGES_PER_STEP = 16
SSM_TC = 1024


def kernel(x_prompt, x_sample, cache_k, cache_v, state_ssm_re, state_ssm_im, page_table, g_ffn1_pre, w_ffn1_gate, w_ffn1_up, w_ffn1_down, g_ffn1_post, g_mix_pre, w_in, sb_bias, lambda_re, lambda_im, log_dt, b_re, b_im, c_re, c_im, d_skip, w_glu, g_attn_out, g_ssm_out, w_out, g_mix_post, g_ffn2_pre, w_ffn2_gate, w_ffn2_up, w_ffn2_down, g_ffn2_post):
    depth = w_in.shape[0]
    bp, tp, d = x_prompt.shape
    bs, ts, _ = x_sample.shape
    groups, states = lambda_re.shape[1:]
    pool, page, heads = cache_k.shape[1:4]
    width = heads * HEAD_DIM
    ck = cache_k.reshape(depth, pool, page * heads, HEAD_DIM)
    cv = cache_v.reshape(depth, pool, page * heads, HEAD_DIM)

    hp = x_prompt.reshape(bp * tp, d)
    hs = x_sample.reshape(bs * ts, d)
    kp = jnp.zeros((depth, bp * tp, width), F32)
    vp = jnp.zeros((depth, bp * tp, width), F32)
    ks = jnp.zeros((depth, bs * ts, width), F32)
    vs = jnp.zeros((depth, bs * ts, width), F32)
    zero_state = jnp.zeros((bp, groups // GROUPS_PER_BLOCK, 1, 2 * GROUPS_PER_BLOCK * states), F32)
    states_out = [[] for _ in range(4)]
    tc_p = min(SSM_TC, tp)
    tc_s = min(SSM_TC, ts)

    gains = lambda g: g.reshape(depth, 1, -1)
    ffn1 = (gains(g_ffn1_pre), w_ffn1_gate, w_ffn1_up, w_ffn1_down, gains(g_ffn1_post))
    ffn2 = (gains(g_ffn2_pre), w_ffn2_gate, w_ffn2_up, w_ffn2_down, gains(g_ffn2_post))
    g_in = gains(g_mix_pre)
    w_in_b = w_in.astype(BF16)
    mixw = (w_glu.astype(BF16), gains(g_attn_out), gains(g_ssm_out), w_out.astype(BF16),
            gains(g_mix_post))
    s5, lam_bar = jax.vmap(_ssm_params)(lambda_re, lambda_im, log_dt, b_re, b_im, c_re, c_im,
                                        d_skip)
    tbl_p = jax.vmap(functools.partial(_segment_table, seg=tc_p // SUBLANES))(lam_bar)
    tbl_s = jax.vmap(functools.partial(_segment_table, seg=tc_s // SUBLANES))(lam_bar)

    for l in range(depth):

        hp, hs = _ffn(hp, hs, *ffn1, l, tm=FFN_TM, tf=FFN_TF)

        qup, kp, vp, qkv16 = _inproj(hp, g_in, w_in_b, kp, vp, l, tm=PROJ_TM)
        qus, ks, vs, _ = _inproj(hs, g_in, w_in_b, ks, vs, l, tm=PROJ_TM)
        qup = qup.reshape(2, bp, tp, width)
        qus = qus.reshape(2, bs, ts, width)

        attn_p = _attn_prompt(qkv16.reshape(3, bp, tp, width), sb_bias[l],
                              tq=ATTN_TQ, hps=ATTN_HEADS_PER_STEP)
        attn_s = _attn_sample(qus[0], ks[l].reshape(bs, ts, width), vs[l].reshape(bs, ts, width),
                              ck, cv, l, page_table, sb_bias[l], ppb=PAGES_PER_STEP)

        yp, hlp = _ssm(qup, zero_state, s5, tbl_p, l, tc=tc_p)
        ys, hls = _ssm(qus, _pack_state(state_ssm_re[l], state_ssm_im[l]), s5, tbl_s, l, tc=tc_s)

        hp = _mix(hp, attn_p.reshape(bp * tp, width), yp.reshape(bp * tp, -1), *mixw, l, tm=MIX_TM)
        hs = _mix(hs, attn_s.reshape(bs * ts, width), ys.reshape(bs * ts, -1), *mixw, l, tm=MIX_TM)

        hp, hs = _ffn(hp, hs, *ffn2, l, tm=FFN_TM, tf=FFN_TF)

        for dst, val in zip(states_out, _unpack_state(hlp, groups, states)
                            + _unpack_state(hls, groups, states)):
            dst.append(val)

    return (hp.reshape(bp, tp, d), hs.reshape(bs, ts, d),
            kp.reshape(depth, bp, tp, heads, HEAD_DIM), vp.reshape(depth, bp, tp, heads, HEAD_DIM),
            ks.reshape(depth, bs, ts, heads, HEAD_DIM), vs.reshape(depth, bs, ts, heads, HEAD_DIM),
            ) + tuple(jnp.stack(o) for o in states_out)
```

```python
import functools
import math

import jax
import jax.numpy as jnp
from jax import lax
from jax.experimental import pallas as pl
from jax.experimental.pallas import tpu as pltpu

F32 = jnp.float32
BF16 = jnp.bfloat16

RMS_EPS = 1e-6
HEAD_DIM = 128
MXU_DIM = 256
SSM_GROUP = 16
SSM_STATE = 64
GROUPS_PER_BLOCK = 8
SUBLANES = 8
VMEM_LIMIT = 62 * 1024 * 1024


def _params(*sem):
    return pltpu.CompilerParams(dimension_semantics=sem, vmem_limit_bytes=VMEM_LIMIT)


def _rms(x, g):
    return x * lax.rsqrt(jnp.mean(x * x, axis=-1, keepdims=True) + RMS_EPS) * g


LOG2E = math.log2(math.e)


EXP2_GUARD = 100.0


def _softplus2(z2):
    return jnp.where(z2 > EXP2_GUARD, z2, jnp.log2(1.0 + jnp.exp2(z2)))


def _nt_dot(a, b):
    return lax.dot_general(a, b, (((1,), (1,)), ((), ())), preferred_element_type=F32)


def _suffix_matrix(s):
    r = lax.broadcasted_iota(jnp.int32, (2 * s, s), 0)
    c = lax.broadcasted_iota(jnp.int32, (2 * s, s), 1)
    return jnp.where((r > c) & ((r < s) | (r > c + s)), 1.0, 0.0).astype(BF16)


def _sb_split(z, s, causal=None):
    m, w = z.shape
    n = w // s
    sp = _softplus2(z)
    log_beta = z - sp
    if causal is not None:
        sp = jnp.where(causal, sp, 0.0)
    if n > 1:
        sp = jnp.concatenate([sp[:, i * s:(i + 1) * s] for i in range(n)], axis=0)
    hi = sp.astype(BF16)
    lo = (sp - hi.astype(F32)).astype(BF16)
    return log_beta, sp[:, :1], jnp.concatenate([hi, lo], axis=1)


def _sb_finish(log_beta, sp0, after, carry, causal=None):
    m, w = log_beta.shape
    n = after.shape[0] // m
    total = after[:, :1] + sp0
    parts = [None] * n
    for i in reversed(range(n)):
        parts[i] = after[i * m:(i + 1) * m] + carry
        carry = carry + total[i * m:(i + 1) * m]
    after = parts[0] if n == 1 else jnp.concatenate(parts, axis=1)
    wgt = jnp.exp2(log_beta - after)
    if causal is not None:
        wgt = jnp.where(causal, wgt, 0.0)
    return wgt.astype(BF16), carry


def _ffn_kernel(x_ref, xs_ref, gpre_ref, wg_ref, wu_ref, wd_ref, gpost_ref, o_ref, os_ref, n_ref):
    i = pl.program_id(0)
    f = pl.program_id(1)
    last = pl.num_programs(1) - 1
    tm = x_ref.shape[0]

    @pl.when(f == 0)
    def _():
        n_ref[:tm, :] = _rms(x_ref[...], gpre_ref[...]).astype(BF16)
        o_ref[...] = jnp.zeros_like(o_ref)

    @pl.when((f == 0) & (i == 0))
    def _():
        n_ref[tm:, :] = _rms(xs_ref[...], gpre_ref[...]).astype(BF16)
        os_ref[...] = jnp.zeros_like(os_ref)

    def swiglu(n):
        gate = jnp.dot(n, wg_ref[...].astype(BF16), preferred_element_type=F32)
        up = jnp.dot(n, wu_ref[...].astype(BF16), preferred_element_type=F32)
        act = gate * jax.nn.sigmoid(gate) * up
        return jnp.dot(act.astype(BF16), wd_ref[...].astype(BF16), preferred_element_type=F32)

    @pl.when(i == 0)
    def _():
        res = swiglu(n_ref[...])
        o_ref[...] += res[:tm]
        os_ref[...] += res[tm:]

    @pl.when(i > 0)
    def _():
        o_ref[...] += swiglu(n_ref[:tm, :])

    @pl.when(f == last)
    def _():
        o_ref[...] = x_ref[...] + 0.5 * _rms(o_ref[...], gpost_ref[...])

    @pl.when((f == last) & (i == 0))
    def _():
        os_ref[...] = xs_ref[...] + 0.5 * _rms(os_ref[...], gpost_ref[...])


def _ffn(x, xs, g_pre, w_gate, w_up, w_down, g_post, layer, *, tm, tf):
    m, d = x.shape
    ms = xs.shape[0]
    dff = w_gate.shape[2]
    tm = min(tm, m)
    return pl.pallas_call(
        _ffn_kernel,
        grid=(m // tm, dff // tf),
        in_specs=[
            pl.BlockSpec((tm, d), lambda i, f: (i, 0)),
            pl.BlockSpec((ms, d), lambda i, f: (0, 0)),
            pl.BlockSpec((None, 1, d), lambda i, f: (layer, 0, 0)),
            pl.BlockSpec((None, d, tf), lambda i, f: (layer, 0, f)),
            pl.BlockSpec((None, d, tf), lambda i, f: (layer, 0, f)),
            pl.BlockSpec((None, tf, d), lambda i, f: (layer, f, 0)),
            pl.BlockSpec((None, 1, d), lambda i, f: (layer, 0, 0)),
        ],
        out_specs=[pl.BlockSpec((tm, d), lambda i, f: (i, 0)),
                   pl.BlockSpec((ms, d), lambda i, f: (0, 0))],
        out_shape=[jax.ShapeDtypeStruct((m, d), F32), jax.ShapeDtypeStruct((ms, d), F32)],
        scratch_shapes=[pltpu.VMEM((tm + ms, d), BF16)],
        compiler_params=_params("arbitrary", "arbitrary"),
        name="ffn_half",
    )(x, xs, g_pre, w_gate, w_up, w_down, g_post)


def _inproj_kernel(x_ref, g_ref, w_ref, kin_ref, vin_ref, qu_ref, k_ref, v_ref, qkv16_ref, n_ref,
                   *, q_scale):
    del kin_ref, vin_ref
    j = pl.program_id(1)

    @pl.when(j == 0)
    def _():
        n_ref[...] = _rms(x_ref[...], g_ref[...]).astype(BF16)

    res = jnp.dot(n_ref[...], w_ref[...], preferred_element_type=F32)

    @pl.when((j == 0) | (j == 3))
    def _():
        qu_ref[...] = res

    @pl.when(j == 0)
    def _():
        qkv16_ref[...] = (res * q_scale).astype(BF16)

    @pl.when(j == 1)
    def _():
        k_ref[...] = res
        qkv16_ref[...] = res.astype(BF16)

    @pl.when(j == 2)
    def _():
        v_ref[...] = res
        qkv16_ref[...] = res.astype(BF16)


def _inproj(x, g, w_in, k_all, v_all, layer, *, tm):
    m, d = x.shape
    width = w_in.shape[2] // 4
    tm = min(tm, m)
    return pl.pallas_call(
        functools.partial(_inproj_kernel, q_scale=HEAD_DIM ** -0.5 * LOG2E),
        grid=(m // tm, 4),
        in_specs=[
            pl.BlockSpec((tm, d), lambda i, j: (i, 0)),
            pl.BlockSpec((None, 1, d), lambda i, j: (layer, 0, 0)),
            pl.BlockSpec((None, d, width), lambda i, j: (layer, 0, j)),
            pl.BlockSpec(memory_space=pl.ANY),
            pl.BlockSpec(memory_space=pl.ANY),
        ],
        out_specs=[
            pl.BlockSpec((None, tm, width), lambda i, j: (j // 3, i, 0)),
            pl.BlockSpec((None, tm, width), lambda i, j: (layer, i, 0)),
            pl.BlockSpec((None, tm, width), lambda i, j: (layer, i, 0)),
            pl.BlockSpec((None, tm, width), lambda i, j: (jnp.minimum(j, 2), i, 0)),
        ],
        out_shape=[jax.ShapeDtypeStruct((2, m, width), F32),
                   jax.ShapeDtypeStruct(k_all.shape, F32),
                   jax.ShapeDtypeStruct(v_all.shape, F32),
                   jax.ShapeDtypeStruct((3, m, width), BF16)],
        input_output_aliases={3: 1, 4: 2},
        scratch_shapes=[pltpu.VMEM((tm, d), BF16)],
        compiler_params=_params("parallel", "arbitrary"),
        name="in_proj",
    )(x, g, w_in, k_all, v_all)


def _attn_kernel(bias_ref, q_ref, k_ref, v_ref, o_ref, *, tq, hps):
    hb = pl.program_id(1)
    i = pl.program_id(2)
    sub = min(tq, MXU_DIM)
    suffix = _suffix_matrix(sub)
    row = lax.broadcasted_iota(jnp.int32, (tq, tq), 0)
    col = lax.broadcasted_iota(jnp.int32, (tq, tq), 1)
    lanes = [slice(n * HEAD_DIM, (n + 1) * HEAD_DIM) for n in range(hps)]
    biases = [bias_ref[hb * hps + n] * LOG2E for n in range(hps)]

    def step(j, state, masked):
        keys = pl.ds(pl.multiple_of(j * tq, tq), tq)
        causal = (col < row) if masked else None
        zs = [_nt_dot(q_ref[:, ln], k_ref[keys, ln]) + biases[n] for n, ln in enumerate(lanes)]
        split = [_sb_split(z, sub, causal) for z in zs]
        afters = [jnp.dot(st, suffix, preferred_element_type=F32) for _, _, st in split]
        new = []
        for n, ln in enumerate(lanes):
            carry, acc = state[n]
            w, carry = _sb_finish(split[n][0], split[n][1], afters[n], carry, causal)
            new.append((carry, acc + jnp.dot(w, v_ref[keys, ln], preferred_element_type=F32)))
        return tuple(new)

    state = tuple((jnp.zeros((tq, 1), F32), jnp.zeros((tq, HEAD_DIM), F32)) for _ in lanes)
    state = step(i, state, True)
    state = lax.fori_loop(0, i, lambda t, s: step(i - 1 - t, s, False), state)
    for n, ln in enumerate(lanes):
        o_ref[:, ln] = state[n][1]


def _attn_prompt(qkv16, sb_bias, *, tq, hps):
    _, b, t, width = qkv16.shape
    heads = width // HEAD_DIM
    tq = min(tq, t)
    hps = min(hps, heads)
    wide = hps * HEAD_DIM
    return pl.pallas_call(
        functools.partial(_attn_kernel, tq=tq, hps=hps),
        grid=(b, heads // hps, t // tq),
        in_specs=[
            pl.BlockSpec(memory_space=pltpu.SMEM),
            pl.BlockSpec((None, None, tq, wide), lambda bi, h, i: (0, bi, i, h)),
            pl.BlockSpec((None, None, t, wide), lambda bi, h, i: (1, bi, 0, h)),
            pl.BlockSpec((None, None, t, wide), lambda bi, h, i: (2, bi, 0, h)),
        ],
        out_specs=pl.BlockSpec((None, tq, wide), lambda bi, h, i: (bi, i, h)),
        out_shape=jax.ShapeDtypeStruct((b, t, width), F32),
        compiler_params=_params("parallel", "parallel", "arbitrary"),
        name="attn_prompt",
    )(sb_bias, qkv16, qkv16, qkv16)


def _sattn_kernel(pt_ref, qbd_ref, bias_ref, kn_ref, vn_ref, *rest,
                  scale, heads, tsteps, ppb):
    del pt_ref
    kc_refs, vc_refs = rest[:ppb], rest[ppb:2 * ppb]
    o_ref, acc_ref, carry_ref = rest[2 * ppb:]
    j = pl.program_id(1)
    rows = carry_ref.shape[0]
    page = kn_ref.shape[0]
    qbd = qbd_ref[...]

    def process(kb, vb, bias, causal):
        z = _nt_dot(qbd, kb) * (scale * LOG2E) + bias
        sub = min(z.shape[1], MXU_DIM)
        log_beta, sp0, stack = _sb_split(z, sub, causal)
        after = jnp.dot(stack, _suffix_matrix(sub), preferred_element_type=F32)
        w, carry = _sb_finish(log_beta, sp0, after, carry_ref[:, :1], causal)
        acc_ref[...] += jnp.dot(w, vb, preferred_element_type=F32)
        carry_ref[...] = jnp.broadcast_to(carry, carry_ref.shape)

    @pl.when(j == 0)
    def _():
        acc_ref[...] = jnp.zeros_like(acc_ref)
        carry_ref[...] = jnp.zeros_like(carry_ref)
        row = lax.broadcasted_iota(jnp.int32, (rows, page), 0)
        col = lax.broadcasted_iota(jnp.int32, (rows, page), 1)
        process(kn_ref[...].astype(BF16), vn_ref[...].astype(BF16), bias_ref[:, :page],
                col < (row % tsteps))

    def gather(ref):
        return jnp.concatenate(
            [ref[pl.ds(h, page, stride=heads), :].astype(BF16) for h in range(heads)], axis=1)

    kb = jnp.concatenate([gather(r) for r in kc_refs], axis=0)
    vb = jnp.concatenate([gather(r) for r in vc_refs], axis=0)
    process(kb, vb, bias_ref[...], None)

    @pl.when(j == pl.num_programs(1) - 1)
    def _():
        for h in range(heads):
            o_ref[:, h * HEAD_DIM:(h + 1) * HEAD_DIM] = (
                acc_ref[h * tsteps:(h + 1) * tsteps, h * HEAD_DIM:(h + 1) * HEAD_DIM])


def _attn_sample(q, k_new, v_new, cache_k, cache_v, layer, page_table, sb_bias, *, ppb):
    b, ts, width = q.shape
    heads = width // HEAD_DIM
    page = cache_k.shape[2] // heads
    n_pages = page_table.shape[1]
    ppb = min(ppb, n_pages)
    rows = heads * ts
    q = q.reshape(b, ts, heads, HEAD_DIM)
    eye = jnp.eye(heads, dtype=F32)
    qbd = (q.transpose(0, 2, 1, 3)[:, :, :, None, :] * eye[None, :, None, :, None])
    qbd = qbd.reshape(b, rows, width).astype(BF16)
    bias = jnp.broadcast_to(jnp.repeat(sb_bias * LOG2E, ts)[:, None], (rows, ppb * page))
    pad = ((0, 0), (0, page - ts), (0, 0))
    k_new = jnp.pad(k_new, pad)
    v_new = jnp.pad(v_new, pad)
    kern = functools.partial(_sattn_kernel, scale=HEAD_DIM ** -0.5, heads=heads, tsteps=ts,
                             ppb=ppb)

    def page_spec(r):
        return pl.BlockSpec(
            (None, None, page * heads, HEAD_DIM),
            lambda bi, j, pt: (layer, pt[bi, n_pages - ppb * (j + 1) + r], 0, 0))

    return pl.pallas_call(
        kern,
        grid_spec=pltpu.PrefetchScalarGridSpec(
            num_scalar_prefetch=1,
            grid=(b, n_pages // ppb),
            in_specs=[
                pl.BlockSpec((None, rows, width), lambda bi, j, pt: (bi, 0, 0)),
                pl.BlockSpec((rows, ppb * page), lambda bi, j, pt: (0, 0)),
                pl.BlockSpec((None, page, width), lambda bi, j, pt: (bi, 0, 0)),
                pl.BlockSpec((None, page, width), lambda bi, j, pt: (bi, 0, 0)),
            ] + [page_spec(r) for r in range(ppb)] * 2,
            out_specs=pl.BlockSpec((None, ts, width), lambda bi, j, pt: (bi, 0, 0)),
            scratch_shapes=[pltpu.VMEM((rows, width), F32), pltpu.VMEM((rows, HEAD_DIM), F32)],
        ),
        out_shape=jax.ShapeDtypeStruct((b, ts, width), F32),
        compiler_params=_params("parallel", "arbitrary"),
        name="attn_sample",
    )(page_table, qbd, bias, k_new, v_new, *([cache_k] * ppb), *([cache_v] * ppb))


def _cmul_add(xr, xi, ar, ai, br, bi):
    return xr + (ar * br - ai * bi), xi + (ar * bi + ai * br)


def _ssm_kernel(u_ref, h0_ref, bbd_ref, cbd_ref, d_ref, lam_ref, tbl_ref, y_ref, hl_ref,
                pad_ref, up_ref, s_ref, carry_ref):
    for b in range(u_ref.shape[0]):
        _ssm_chunk(u_ref.at[b], h0_ref.at[b], bbd_ref, cbd_ref, d_ref, lam_ref, tbl_ref,
                   y_ref.at[b], hl_ref.at[b], pad_ref, up_ref, s_ref, carry_ref.at[b])


def _ssm_chunk(u_ref, h0_ref, bbd_ref, cbd_ref, d_ref, lam_ref, tbl_ref, y_ref, hl_ref,
               pad_ref, up_ref, s_ref, carry_ref):
    t = pl.program_id(2)
    tc, lanes = s_ref.shape
    half = lanes // 2
    seg = tc // SUBLANES
    unroll = min(seg, 8)
    rows8 = lambda k: pl.ds(pl.multiple_of(k * SUBLANES, SUBLANES), SUBLANES)

    @pl.when(t == 0)
    def _():
        carry_ref[...] = jnp.broadcast_to(h0_ref[...], carry_ref.shape)

    if seg > 1:
        pitch = seg + SUBLANES
        for j in range(SUBLANES):
            pad_ref[j * pitch:j * pitch + seg, :] = u_ref[j * seg:(j + 1) * seg, :]

        def permute(k, _):
            up_ref[rows8(k), :] = pad_ref[pl.ds(k, SUBLANES, stride=pitch), :]
            return _
        lax.fori_loop(0, seg, permute, 0, unroll=unroll)
        u = up_ref[...]
    else:
        u = u_ref[...]
    s_ref[...] = jnp.dot(u.astype(BF16), bbd_ref[...], preferred_element_type=F32)

    lam_r = jnp.broadcast_to(lam_ref[:, :half], (SUBLANES, half))
    lam_i = jnp.broadcast_to(lam_ref[:, half:], (SUBLANES, half))

    def local(k, c):
        return _cmul_add(s_ref[rows8(k), :half], s_ref[rows8(k), half:], lam_r, lam_i, *c)

    zero = jnp.zeros((SUBLANES, half), F32)
    xr, xi = lax.fori_loop(0, seg, local, (zero, zero), unroll=unroll)

    for n, d in enumerate((1, 2, 4)):
        xr, xi = _cmul_add(xr, xi, tbl_ref[2 * n], tbl_ref[2 * n + 1],
                           pltpu.roll(xr, d, 0), pltpu.roll(xi, d, 0))
    cr = carry_ref[:, :half]
    ci = carry_ref[:, half:]
    fr, fi = _cmul_add(xr, xi, tbl_ref[6], tbl_ref[7], cr, ci)
    first = lax.broadcasted_iota(jnp.int32, (SUBLANES, half), 0) == 0
    sr = jnp.where(first, cr, pltpu.roll(fr, 1, 0))
    si = jnp.where(first, ci, pltpu.roll(fi, 1, 0))
    last = SUBLANES - 1
    carry_ref[:, :half] = jnp.broadcast_to(fr[last:, :], (SUBLANES, half))
    carry_ref[:, half:] = jnp.broadcast_to(fi[last:, :], (SUBLANES, half))

    def true_scan(k, c):
        hr, hi = _cmul_add(s_ref[rows8(k), :half], s_ref[rows8(k), half:], lam_r, lam_i, *c)
        s_ref[rows8(k), :half] = hr
        s_ref[rows8(k), half:] = hi
        return hr, hi

    lax.fori_loop(0, seg, true_scan, (sr, si), unroll=unroll)

    y = (jnp.dot(s_ref[...].astype(BF16), cbd_ref[...], preferred_element_type=F32)
         + d_ref[...] * u)
    if seg > 1:
        up_ref[...] = y

        def unpermute(k, _):
            y_ref[pl.ds(k, SUBLANES, stride=seg), :] = up_ref[rows8(k), :]
            return _
        lax.fori_loop(0, seg, unpermute, 0, unroll=unroll)
    else:
        y_ref[...] = y

    @pl.when(t == pl.num_programs(2) - 1)
    def _():
        hl_ref[...] = carry_ref[:1, :]


def _ssm_params(lam_re, lam_im, log_dt, b_re, b_im, c_re, c_im, d_skip):
    g, p = lam_re.shape
    c = b_re.shape[-1]
    nb = g // GROUPS_PER_BLOCK
    gb = GROUPS_PER_BLOCK
    lam = lax.complex(lam_re, lam_im)
    dt = jnp.exp(log_dt)[:, None]
    lam_bar = jnp.exp(lam * dt)
    b_bar = ((lam_bar - 1.0) / lam)[..., None] * lax.complex(b_re, b_im)
    eye = jnp.eye(gb, dtype=F32)

    def blockdiag_in(x):
        x = x.reshape(nb, gb, p, c).transpose(0, 1, 3, 2)
        x = x[:, :, :, None, :] * eye[None, :, None, :, None]
        return x.reshape(nb, gb * c, gb * p)

    bbd = jnp.concatenate([blockdiag_in(b_bar.real), blockdiag_in(b_bar.imag)], axis=-1)

    def blockdiag_out(x):
        x = x.reshape(nb, gb, c, p).transpose(0, 1, 3, 2)
        x = x[:, :, :, None, :] * eye[None, :, None, :, None]
        return x.reshape(nb, gb * p, gb * c)

    cbd = jnp.concatenate([blockdiag_out(c_re), -blockdiag_out(c_im)], axis=1)

    lam_blk = lam_bar.reshape(nb, 1, gb * p)
    lam_blk = jnp.concatenate([lam_blk.real, lam_blk.imag], axis=-1)
    dsk = d_skip.reshape(nb, 1, gb * c)
    return (bbd.astype(BF16), cbd.astype(BF16), dsk, lam_blk), lam_bar


def _segment_table(lam_bar, seg):
    g, p = lam_bar.shape
    nb = g // GROUPS_PER_BLOCK
    assert seg & (seg - 1) == 0
    base = lam_bar
    for _ in range(seg.bit_length() - 1):
        base = base * base
    pw = [base]
    for _ in range(SUBLANES - 1):
        pw.append(pw[-1] * base)
    pw = jnp.stack(pw)
    rows = jnp.arange(SUBLANES)[:, None, None]
    tbls = []
    for d in (1, 2, 4):
        m = jnp.where(rows >= d, jnp.broadcast_to(pw[d - 1], pw.shape), 0.0)
        tbls += [m.real, m.imag]
    tbls += [pw.real, pw.imag]
    tbl = jnp.stack(tbls)
    return tbl.reshape(8, SUBLANES, nb, GROUPS_PER_BLOCK * p).transpose(2, 0, 1, 3)


def _ssm(qu, h0, params, tbl, layer, *, tc):
    bbd, cbd, dsk, lam = params
    _, b, t, width = qu.shape
    _, nb, uw, sw = bbd.shape
    assert t % tc == 0 and tc % SUBLANES == 0
    seg = tc // SUBLANES
    bb = b if t == tc else 1
    lead = None if bb == 1 else bb
    carry = (SUBLANES, sw) if bb == 1 else (bb, SUBLANES, sw)
    return pl.pallas_call(
        _ssm_chunk if bb == 1 else _ssm_kernel,
        grid=(b // bb, nb, t // tc),
        in_specs=[
            pl.BlockSpec((None, lead, tc, uw), lambda bi, g, ti: (1, bi, ti, g)),
            pl.BlockSpec((lead, None, 1, sw), lambda bi, g, ti: (bi, g, 0, 0)),
            pl.BlockSpec((None, None, uw, sw), lambda bi, g, ti: (layer, g, 0, 0)),
            pl.BlockSpec((None, None, sw, uw), lambda bi, g, ti: (layer, g, 0, 0)),
            pl.BlockSpec((None, None, 1, uw), lambda bi, g, ti: (layer, g, 0, 0)),
            pl.BlockSpec((None, None, 1, sw), lambda bi, g, ti: (layer, g, 0, 0)),
            pl.BlockSpec((None, None, 8, SUBLANES, sw // 2),
                         lambda bi, g, ti: (layer, g, 0, 0, 0)),
        ],
        out_specs=[
            pl.BlockSpec((lead, tc, uw), lambda bi, g, ti: (bi, ti, g)),
            pl.BlockSpec((lead, None, 1, sw), lambda bi, g, ti: (bi, g, 0, 0)),
        ],
        out_shape=[jax.ShapeDtypeStruct((b, t, width), F32),
                   jax.ShapeDtypeStruct((b, nb, 1, sw), F32)],
        scratch_shapes=[pltpu.VMEM((SUBLANES * (seg + SUBLANES), uw), F32),
                        pltpu.VMEM((tc, uw), F32), pltpu.VMEM((tc, sw), F32),
                        pltpu.VMEM(carry, F32)],
        compiler_params=_params("parallel", "parallel", "arbitrary"),
        name="s5_scan",
    )(qu, h0, bbd, cbd, dsk, lam, tbl)


def _pack_state(re, im):
    b, g, p = re.shape
    nb = g // GROUPS_PER_BLOCK
    re = re.reshape(b, nb, 1, GROUPS_PER_BLOCK * p)
    im = im.reshape(b, nb, 1, GROUPS_PER_BLOCK * p)
    return jnp.concatenate([re, im], axis=-1)


def _unpack_state(hl, g, p):
    b = hl.shape[0]
    half = hl.shape[-1] // 2
    return hl[..., :half].reshape(b, g, p), hl[..., half:].reshape(b, g, p)


def _mix_kernel(h_ref, a_ref, y_ref, wglu_ref, gattn_ref, gssm_ref, wout_ref, gpost_ref, o_ref):
    y = y_ref[...]
    g = 0.5 * y * (1.0 + jnp.tanh(math.sqrt(2.0 / math.pi) * (y + 0.044715 * (y * y * y))))
    gate = jnp.dot(g.astype(BF16), wglu_ref[...], preferred_element_type=F32)
    ssm_out = g * jax.nn.sigmoid(gate)
    na = _rms(a_ref[...], gattn_ref[...]).astype(BF16)
    ns = _rms(ssm_out, gssm_ref[...]).astype(BF16)
    aw = na.shape[1]
    m = (jnp.dot(na, wout_ref[:aw, :], preferred_element_type=F32)
         + jnp.dot(ns, wout_ref[aw:, :], preferred_element_type=F32))
    o_ref[...] = h_ref[...] + _rms(m, gpost_ref[...])


def _mix(h, attn, y, w_glu, g_attn, g_ssm, w_out, g_post, layer, *, tm):
    m, d = h.shape
    aw = attn.shape[1]
    sw = y.shape[1]
    tm = min(tm, m)
    return pl.pallas_call(
        _mix_kernel,
        grid=(m // tm,),
        in_specs=[
            pl.BlockSpec((tm, d), lambda i: (i, 0)),
            pl.BlockSpec((tm, aw), lambda i: (i, 0)),
            pl.BlockSpec((tm, sw), lambda i: (i, 0)),
            pl.BlockSpec((None, sw, sw), lambda i: (layer, 0, 0)),
            pl.BlockSpec((None, 1, aw), lambda i: (layer, 0, 0)),
            pl.BlockSpec((None, 1, sw), lambda i: (layer, 0, 0)),
            pl.BlockSpec((None, d, d), lambda i: (layer, 0, 0)),
            pl.BlockSpec((None, 1, d), lambda i: (layer, 0, 0)),
        ],
        out_specs=pl.BlockSpec((tm, d), lambda i: (i, 0)),
        out_shape=jax.ShapeDtypeStruct((m, d), F32),
        compiler_params=_params("parallel"),
        name="mix_out",
    )(h, attn, y, w_glu, g_attn, g_ssm, w_out, g_post)


FFN_TM = 1024
FFN_TF = 256
PROJ_TM = 512
MIX_TM = 512
ATTN_TQ = 256
ATTN_HEADS_PER_STEP = 8
PAGES_PER_STEP = 16
SSM_TC = 1024


def kernel(x_prompt, x_sample, cache_k, cache_v, state_ssm_re, state_ssm_im, page_table, g_ffn1_pre, w_ffn1_gate, w_ffn1_up, w_ffn1_down, g_ffn1_post, g_mix_pre, w_in, sb_bias, lambda_re, lambda_im, log_dt, b_re, b_im, c_re, c_im, d_skip, w_glu, g_attn_out, g_ssm_out, w_out, g_mix_post, g_ffn2_pre, w_ffn2_gate, w_ffn2_up, w_ffn2_down, g_ffn2_post):
    depth = w_in.shape[0]
    bp, tp, d = x_prompt.shape
    bs, ts, _ = x_sample.shape
    groups, states = lambda_re.shape[1:]
    pool, page, heads = cache_k.shape[1:4]
    width = heads * HEAD_DIM
    ck = cache_k.reshape(depth, pool, page * heads, HEAD_DIM)
    cv = cache_v.reshape(depth, pool, page * heads, HEAD_DIM)

    hp = x_prompt.reshape(bp * tp, d)
    hs = x_sample.reshape(bs * ts, d)
    kp = jnp.zeros((depth, bp * tp, width), F32)
    vp = jnp.zeros((depth, bp * tp, width), F32)
    ks = jnp.zeros((depth, bs * ts, width), F32)
    vs = jnp.zeros((depth, bs * ts, width), F32)
    zero_state = jnp.zeros((bp, groups // GROUPS_PER_BLOCK, 1, 2 * GROUPS_PER_BLOCK * states), F32)
    states_out = [[] for _ in range(4)]
    tc_p = min(SSM_TC, tp)
    tc_s = min(SSM_TC, ts)

    gains = lambda g: g.reshape(depth, 1, -1)
    ffn1 = (gains(g_ffn1_pre), w_ffn1_gate, w_ffn1_up, w_ffn1_down, gains(g_ffn1_post))
    ffn2 = (gains(g_ffn2_pre), w_ffn2_gate, w_ffn2_up, w_ffn2_down, gains(g_ffn2_post))
    g_in = gains(g_mix_pre)
    w_in_b = w_in.astype(BF16)
    mixw = (w_glu.astype(BF16), gains(g_attn_out), gains(g_ssm_out), w_out.astype(BF16),
            gains(g_mix_post))
    s5, lam_bar = jax.vmap(_ssm_params)(lambda_re, lambda_im, log_dt, b_re, b_im, c_re, c_im,
                                        d_skip)
    tbl_p = jax.vmap(functools.partial(_segment_table, seg=tc_p // SUBLANES))(lam_bar)
    tbl_s = jax.vmap(functools.partial(_segment_table, seg=tc_s // SUBLANES))(lam_bar)

    for l in range(depth):

        hp, hs = _ffn(hp, hs, *ffn1, l, tm=FFN_TM, tf=FFN_TF)

        qup, kp, vp, qkv16 = _inproj(hp, g_in, w_in_b, kp, vp, l, tm=PROJ_TM)
        qus, ks, vs, _ = _inproj(hs, g_in, w_in_b, ks, vs, l, tm=PROJ_TM)
        qup = qup.reshape(2, bp, tp, width)
        qus = qus.reshape(2, bs, ts, width)

        attn_p = _attn_prompt(qkv16.reshape(3, bp, tp, width), sb_bias[l],
                              tq=ATTN_TQ, hps=ATTN_HEADS_PER_STEP)
        attn_s = _attn_sample(qus[0], ks[l].reshape(bs, ts, width), vs[l].reshape(bs, ts, width),
                              ck, cv, l, page_table, sb_bias[l], ppb=PAGES_PER_STEP)

        yp, hlp = _ssm(qup, zero_state, s5, tbl_p, l, tc=tc_p)
        ys, hls = _ssm(qus, _pack_state(state_ssm_re[l], state_ssm_im[l]), s5, tbl_s, l, tc=tc_s)

        hp = _mix(hp, attn_p.reshape(bp * tp, width), yp.reshape(bp * tp, -1), *mixw, l, tm=MIX_TM)
        hs = _mix(hs, attn_s.reshape(bs * ts, width), ys.reshape(bs * ts, -1), *mixw, l, tm=MIX_TM)

        hp, hs = _ffn(hp, hs, *ffn2, l, tm=FFN_TM, tf=FFN_TF)

        for dst, val in zip(states_out, _unpack_state(hlp, groups, states)
                            + _unpack_state(hls, groups, states)):
            dst.append(val)

    return (hp.reshape(bp, tp, d), hs.reshape(bs, ts, d),
            kp.reshape(depth, bp, tp, heads, HEAD_DIM), vp.reshape(depth, bp, tp, heads, HEAD_DIM),
            ks.reshape(depth, bs, ts, heads, HEAD_DIM), vs.reshape(depth, bs, ts, heads, HEAD_DIM),
            ) + tuple(jnp.stack(o) for o in states_out)
```

```python
import functools
import math

import jax
import jax.numpy as jnp
from jax import lax
from jax.experimental import pallas as pl
from jax.experimental.pallas import tpu as pltpu

F32 = jnp.float32
BF16 = jnp.bfloat16

RMS_EPS = 1e-6
HEAD_DIM = 128
MXU_DIM = 256
SSM_GROUP = 16
SSM_STATE = 64
GROUPS_PER_BLOCK = 8
SUBLANES = 8
VMEM_LIMIT = 62 * 1024 * 1024


def _params(*sem):
    return pltpu.CompilerParams(dimension_semantics=sem, vmem_limit_bytes=VMEM_LIMIT)


def _rms(x, g):
    return x * lax.rsqrt(jnp.mean(x * x, axis=-1, keepdims=True) + RMS_EPS) * g


LOG2E = math.log2(math.e)


EXP2_GUARD = 100.0


def _softplus2(z2):
    return jnp.where(z2 > EXP2_GUARD, z2, jnp.log2(1.0 + jnp.exp2(z2)))


def _nt_dot(a, b):
    return lax.dot_general(a, b, (((1,), (1,)), ((), ())), preferred_element_type=F32)


def _suffix_matrix(s):
    r = lax.broadcasted_iota(jnp.int32, (2 * s, s), 0)
    c = lax.broadcasted_iota(jnp.int32, (2 * s, s), 1)
    return jnp.where((r > c) & ((r < s) | (r > c + s)), 1.0, 0.0).astype(BF16)


def _sb_split(z, s, causal=None):
    m, w = z.shape
    n = w // s
    sp = _softplus2(z)
    log_beta = z - sp
    if causal is not None:
        sp = jnp.where(causal, sp, 0.0)
    if n > 1:
        sp = jnp.concatenate([sp[:, i * s:(i + 1) * s] for i in range(n)], axis=0)
    hi = sp.astype(BF16)
    lo = (sp - hi.astype(F32)).astype(BF16)
    return log_beta, sp[:, :1], jnp.concatenate([hi, lo], axis=1)


def _sb_finish(log_beta, sp0, after, carry, causal=None):
    m, w = log_beta.shape
    n = after.shape[0] // m
    total = after[:, :1] + sp0
    parts = [None] * n
    for i in reversed(range(n)):
        parts[i] = after[i * m:(i + 1) * m] + carry
        carry = carry + total[i * m:(i + 1) * m]
    after = parts[0] if n == 1 else jnp.concatenate(parts, axis=1)
    wgt = jnp.exp2(log_beta - after)
    if causal is not None:
        wgt = jnp.where(causal, wgt, 0.0)
    return wgt.astype(BF16), carry


def _ffn_kernel(x_ref, xs_ref, gpre_ref, wg_ref, wu_ref, wd_ref, gpost_ref, o_ref, os_ref, n_ref):
    i = pl.program_id(0)
    f = pl.program_id(1)
    last = pl.num_programs(1) - 1
    tm = x_ref.shape[0]

    @pl.when(f == 0)
    def _():
        n_ref[:tm, :] = _rms(x_ref[...], gpre_ref[...]).astype(BF16)
        o_ref[...] = jnp.zeros_like(o_ref)

    @pl.when((f == 0) & (i == 0))
    def _():
        n_ref[tm:, :] = _rms(xs_ref[...], gpre_ref[...]).astype(BF16)
        os_ref[...] = jnp.zeros_like(os_ref)

    def swiglu(n):
        gate = jnp.dot(n, wg_ref[...].astype(BF16), preferred_element_type=F32)
        up = jnp.dot(n, wu_ref[...].astype(BF16), preferred_element_type=F32)
        act = gate * jax.nn.sigmoid(gate) * up
        return jnp.dot(act.astype(BF16), wd_ref[...].astype(BF16), preferred_element_type=F32)

    @pl.when(i == 0)
    def _():
        res = swiglu(n_ref[...])
        o_ref[...] += res[:tm]
        os_ref[...] += res[tm:]

    @pl.when(i > 0)
    def _():
        o_ref[...] += swiglu(n_ref[:tm, :])

    @pl.when(f == last)
    def _():
        o_ref[...] = x_ref[...] + 0.5 * _rms(o_ref[...], gpost_ref[...])

    @pl.when((f == last) & (i == 0))
    def _():
        os_ref[...] = xs_ref[...] + 0.5 * _rms(os_ref[...], gpost_ref[...])


def _ffn(x, xs, g_pre, w_gate, w_up, w_down, g_post, layer, *, tm, tf):
    m, d = x.shape
    ms = xs.shape[0]
    dff = w_gate.shape[2]
    tm = min(tm, m)
    return pl.pallas_call(
        _ffn_kernel,
        grid=(m // tm, dff // tf),
        in_specs=[
            pl.BlockSpec((tm, d), lambda i, f: (i, 0)),
            pl.BlockSpec((ms, d), lambda i, f: (0, 0)),
            pl.BlockSpec((None, 1, d), lambda i, f: (layer, 0, 0)),
            pl.BlockSpec((None, d, tf), lambda i, f: (layer, 0, f)),
            pl.BlockSpec((None, d, tf), lambda i, f: (layer, 0, f)),
            pl.BlockSpec((None, tf, d), lambda i, f: (layer, f, 0)),
            pl.BlockSpec((None, 1, d), lambda i, f: (layer, 0, 0)),
        ],
        out_specs=[pl.BlockSpec((tm, d), lambda i, f: (i, 0), pipeline_mode=pl.Buffered(1)),
                   pl.BlockSpec((ms, d), lambda i, f: (0, 0))],
        out_shape=[jax.ShapeDtypeStruct((m, d), F32), jax.ShapeDtypeStruct((ms, d), F32)],
        scratch_shapes=[pltpu.VMEM((tm + ms, d), BF16)],
        compiler_params=_params("arbitrary", "arbitrary"),
        name="ffn_half",
    )(x, xs, g_pre, w_gate, w_up, w_down, g_post)


def _inproj_kernel(x_ref, g_ref, w_ref, kin_ref, vin_ref, qu_ref, k_ref, v_ref, qkv16_ref, n_ref,
                   *, q_scale):
    del kin_ref, vin_ref
    j = pl.program_id(1)

    @pl.when(j == 0)
    def _():
        n_ref[...] = _rms(x_ref[...], g_ref[...]).astype(BF16)

    res = jnp.dot(n_ref[...], w_ref[...], preferred_element_type=F32)

    @pl.when((j == 0) | (j == 3))
    def _():
        qu_ref[...] = res

    @pl.when(j == 0)
    def _():
        qkv16_ref[...] = (res * q_scale).astype(BF16)

    @pl.when(j == 1)
    def _():
        k_ref[...] = res
        qkv16_ref[...] = res.astype(BF16)

    @pl.when(j == 2)
    def _():
        v_ref[...] = res
        qkv16_ref[...] = res.astype(BF16)


def _inproj(x, g, w_in, k_all, v_all, layer, *, tm):
    m, d = x.shape
    width = w_in.shape[2] // 4
    tm = min(tm, m)
    return pl.pallas_call(
        functools.partial(_inproj_kernel, q_scale=HEAD_DIM ** -0.5 * LOG2E),
        grid=(m // tm, 4),
        in_specs=[
            pl.BlockSpec((tm, d), lambda i, j: (i, 0)),
            pl.BlockSpec((None, 1, d), lambda i, j: (layer, 0, 0)),
            pl.BlockSpec((None, d, width), lambda i, j: (layer, 0, j)),
            pl.BlockSpec(memory_space=pl.ANY),
            pl.BlockSpec(memory_space=pl.ANY),
        ],
        out_specs=[
            pl.BlockSpec((None, tm, width), lambda i, j: (j // 3, i, 0)),
            pl.BlockSpec((None, tm, width), lambda i, j: (layer, i, 0)),
            pl.BlockSpec((None, tm, width), lambda i, j: (layer, i, 0)),
            pl.BlockSpec((None, tm, width), lambda i, j: (jnp.minimum(j, 2), i, 0)),
        ],
        out_shape=[jax.ShapeDtypeStruct((2, m, width), F32),
                   jax.ShapeDtypeStruct(k_all.shape, F32),
                   jax.ShapeDtypeStruct(v_all.shape, F32),
                   jax.ShapeDtypeStruct((3, m, width), BF16)],
        input_output_aliases={3: 1, 4: 2},
        scratch_shapes=[pltpu.VMEM((tm, d), BF16)],
        compiler_params=_params("parallel", "arbitrary"),
        name="in_proj",
    )(x, g, w_in, k_all, v_all)


def _attn_kernel(bias_ref, q_ref, k_ref, v_ref, o_ref, *, tq, hps):
    hb = pl.program_id(1)
    i = pl.program_id(2)
    sub = min(tq, MXU_DIM)
    suffix = _suffix_matrix(sub)
    row = lax.broadcasted_iota(jnp.int32, (tq, tq), 0)
    col = lax.broadcasted_iota(jnp.int32, (tq, tq), 1)
    lanes = [slice(n * HEAD_DIM, (n + 1) * HEAD_DIM) for n in range(hps)]
    biases = [bias_ref[hb * hps + n] * LOG2E for n in range(hps)]

    def step(j, nblk, state, masked):
        keys = pl.ds(pl.multiple_of(j * tq, nblk * tq), nblk * tq)
        causal = (col < row) if masked else None
        zs = [_nt_dot(q_ref[:, ln], k_ref[keys, ln]) + biases[n] for n, ln in enumerate(lanes)]
        split = [_sb_split(z, sub, causal) for z in zs]
        afters = [jnp.dot(st, suffix, preferred_element_type=F32) for _, _, st in split]
        new = []
        for n, ln in enumerate(lanes):
            carry, acc = state[n]
            w, carry = _sb_finish(split[n][0], split[n][1], afters[n], carry, causal)
            new.append((carry, acc + jnp.dot(w, v_ref[keys, ln], preferred_element_type=F32)))
        return tuple(new)

    state = tuple((jnp.zeros((tq, 1), F32), jnp.zeros((tq, HEAD_DIM), F32)) for _ in lanes)
    state = step(i, 1, state, True)
    state = lax.fori_loop(0, i % 2, lambda t, s: step(i - 1, 1, s, False), state)
    pairs = i // 2
    state = lax.fori_loop(0, pairs, lambda t, s: step(2 * (pairs - 1 - t), 2, s, False), state)
    for n, ln in enumerate(lanes):
        o_ref[:, ln] = state[n][1]


def _attn_prompt(qkv16, sb_bias, *, tq, hps):
    _, b, t, width = qkv16.shape
    heads = width // HEAD_DIM
    tq = min(tq, t)
    hps = min(hps, heads)
    wide = hps * HEAD_DIM
    return pl.pallas_call(
        functools.partial(_attn_kernel, tq=tq, hps=hps),
        grid=(b, heads // hps, t // tq),
        in_specs=[
            pl.BlockSpec(memory_space=pltpu.SMEM),
            pl.BlockSpec((None, None, tq, wide), lambda bi, h, i: (0, bi, i, h)),
            pl.BlockSpec((None, None, t, wide), lambda bi, h, i: (1, bi, 0, h)),
            pl.BlockSpec((None, None, t, wide), lambda bi, h, i: (2, bi, 0, h)),
        ],
        out_specs=pl.BlockSpec((None, tq, wide), lambda bi, h, i: (bi, i, h)),
        out_shape=jax.ShapeDtypeStruct((b, t, width), F32),
        compiler_params=_params("parallel", "parallel", "arbitrary"),
        name="attn_prompt",
    )(sb_bias, qkv16, qkv16, qkv16)


def _sattn_kernel(pt_ref, qbd_ref, bias_ref, kn_ref, vn_ref, *rest,
                  scale, heads, tsteps, ppb):
    del pt_ref
    kc_refs, vc_refs = rest[:ppb], rest[ppb:2 * ppb]
    o_ref, acc_ref, carry_ref = rest[2 * ppb:]
    j = pl.program_id(1)
    rows = carry_ref.shape[0]
    page = kn_ref.shape[0]
    qbd = qbd_ref[...]

    def process(kb, vb, bias, causal):
        z = _nt_dot(qbd, kb) * (scale * LOG2E) + bias
        sub = min(z.shape[1], MXU_DIM)
        log_beta, sp0, stack = _sb_split(z, sub, causal)
        after = jnp.dot(stack, _suffix_matrix(sub), preferred_element_type=F32)
        w, carry = _sb_finish(log_beta, sp0, after, carry_ref[:, :1], causal)
        acc_ref[...] += jnp.dot(w, vb, preferred_element_type=F32)
        carry_ref[...] = jnp.broadcast_to(carry, carry_ref.shape)

    @pl.when(j == 0)
    def _():
        acc_ref[...] = jnp.zeros_like(acc_ref)
        carry_ref[...] = jnp.zeros_like(carry_ref)
        row = lax.broadcasted_iota(jnp.int32, (rows, page), 0)
        col = lax.broadcasted_iota(jnp.int32, (rows, page), 1)
        process(kn_ref[...].astype(BF16), vn_ref[...].astype(BF16), bias_ref[:, :page],
                col < (row % tsteps))

    def gather(ref):
        return jnp.concatenate(
            [ref[pl.ds(h, page, stride=heads), :].astype(BF16) for h in range(heads)], axis=1)

    kb = jnp.concatenate([gather(r) for r in kc_refs], axis=0)
    vb = jnp.concatenate([gather(r) for r in vc_refs], axis=0)
    process(kb, vb, bias_ref[...], None)

    @pl.when(j == pl.num_programs(1) - 1)
    def _():
        for h in range(heads):
            o_ref[:, h * HEAD_DIM:(h + 1) * HEAD_DIM] = (
                acc_ref[h * tsteps:(h + 1) * tsteps, h * HEAD_DIM:(h + 1) * HEAD_DIM])


def _attn_sample(q, k_new, v_new, cache_k, cache_v, layer, page_table, sb_bias, *, ppb):
    b, ts, width = q.shape
    heads = width // HEAD_DIM
    page = cache_k.shape[2] // heads
    n_pages = page_table.shape[1]
    ppb = min(ppb, n_pages)
    rows = heads * ts
    q = q.reshape(b, ts, heads, HEAD_DIM)
    eye = jnp.eye(heads, dtype=F32)
    qbd = (q.transpose(0, 2, 1, 3)[:, :, :, None, :] * eye[None, :, None, :, None])
    qbd = qbd.reshape(b, rows, width).astype(BF16)
    bias = jnp.broadcast_to(jnp.repeat(sb_bias * LOG2E, ts)[:, None], (rows, ppb * page))
    pad = ((0, 0), (0, page - ts), (0, 0))
    k_new = jnp.pad(k_new, pad)
    v_new = jnp.pad(v_new, pad)
    kern = functools.partial(_sattn_kernel, scale=HEAD_DIM ** -0.5, heads=heads, tsteps=ts,
                             ppb=ppb)

    def page_spec(r):
        return pl.BlockSpec(
            (None, None, page * heads, HEAD_DIM),
            lambda bi, j, pt: (layer, pt[bi, n_pages - ppb * (j + 1) + r], 0, 0))

    return pl.pallas_call(
        kern,
        grid_spec=pltpu.PrefetchScalarGridSpec(
            num_scalar_prefetch=1,
            grid=(b, n_pages // ppb),
            in_specs=[
                pl.BlockSpec((None, rows, width), lambda bi, j, pt: (bi, 0, 0)),
                pl.BlockSpec((rows, ppb * page), lambda bi, j, pt: (0, 0)),
                pl.BlockSpec((None, page, width), lambda bi, j, pt: (bi, 0, 0)),
                pl.BlockSpec((None, page, width), lambda bi, j, pt: (bi, 0, 0)),
            ] + [page_spec(r) for r in range(ppb)] * 2,
            out_specs=pl.BlockSpec((None, ts, width), lambda bi, j, pt: (bi, 0, 0)),
            scratch_shapes=[pltpu.VMEM((rows, width), F32), pltpu.VMEM((rows, HEAD_DIM), F32)],
        ),
        out_shape=jax.ShapeDtypeStruct((b, ts, width), F32),
        compiler_params=_params("parallel", "arbitrary"),
        name="attn_sample",
    )(page_table, qbd, bias, k_new, v_new, *([cache_k] * ppb), *([cache_v] * ppb))


def _cmul_add(xr, xi, ar, ai, br, bi):
    return xr + (ar * br - ai * bi), xi + (ar * bi + ai * br)


def _ssm_kernel(u_ref, h0_ref, bbd_ref, cbd_ref, d_ref, lam_ref, tbl_ref, y_ref, hl_ref,
                pad_ref, up_ref, s_ref, carry_ref):
    for b in range(u_ref.shape[0]):
        _ssm_chunk(u_ref.at[b], h0_ref.at[b], bbd_ref, cbd_ref, d_ref, lam_ref, tbl_ref,
                   y_ref.at[b], hl_ref.at[b], pad_ref, up_ref, s_ref, carry_ref.at[b])


def _ssm_chunk(u_ref, h0_ref, bbd_ref, cbd_ref, d_ref, lam_ref, tbl_ref, y_ref, hl_ref,
               pad_ref, up_ref, s_ref, carry_ref):
    t = pl.program_id(2)
    tc, lanes = s_ref.shape
    half = lanes // 2
    seg = tc // SUBLANES
    unroll = min(seg, 8)
    rows8 = lambda k: pl.ds(pl.multiple_of(k * SUBLANES, SUBLANES), SUBLANES)

    @pl.when(t == 0)
    def _():
        carry_ref[...] = jnp.broadcast_to(h0_ref[...], carry_ref.shape)

    if seg > 1:
        pitch = seg + SUBLANES
        for j in range(SUBLANES):
            pad_ref[j * pitch:j * pitch + seg, :] = u_ref[j * seg:(j + 1) * seg, :]

        def permute(k, _):
            up_ref[rows8(k), :] = pad_ref[pl.ds(k, SUBLANES, stride=pitch), :]
            return _
        lax.fori_loop(0, seg, permute, 0, unroll=unroll)
        u = up_ref[...]
    else:
        u = u_ref[...]
    s_ref[...] = jnp.dot(u.astype(BF16), bbd_ref[...], preferred_element_type=F32)

    lam_r = jnp.broadcast_to(lam_ref[:, :half], (SUBLANES, half))
    lam_i = jnp.broadcast_to(lam_ref[:, half:], (SUBLANES, half))

    def local(k, c):
        return _cmul_add(s_ref[rows8(k), :half], s_ref[rows8(k), half:], lam_r, lam_i, *c)

    zero = jnp.zeros((SUBLANES, half), F32)
    xr, xi = lax.fori_loop(0, seg, local, (zero, zero), unroll=unroll)

    for n, d in enumerate((1, 2, 4)):
        xr, xi = _cmul_add(xr, xi, tbl_ref[2 * n], tbl_ref[2 * n + 1],
                           pltpu.roll(xr, d, 0), pltpu.roll(xi, d, 0))
    cr = carry_ref[:, :half]
    ci = carry_ref[:, half:]
    fr, fi = _cmul_add(xr, xi, tbl_ref[6], tbl_ref[7], cr, ci)
    first = lax.broadcasted_iota(jnp.int32, (SUBLANES, half), 0) == 0
    sr = jnp.where(first, cr, pltpu.roll(fr, 1, 0))
    si = jnp.where(first, ci, pltpu.roll(fi, 1, 0))
    last = SUBLANES - 1
    carry_ref[:, :half] = jnp.broadcast_to(fr[last:, :], (SUBLANES, half))
    carry_ref[:, half:] = jnp.broadcast_to(fi[last:, :], (SUBLANES, half))

    def true_scan(k, c):
        hr, hi = _cmul_add(s_ref[rows8(k), :half], s_ref[rows8(k), half:], lam_r, lam_i, *c)
        s_ref[rows8(k), :half] = hr
        s_ref[rows8(k), half:] = hi
        return hr, hi

    lax.fori_loop(0, seg, true_scan, (sr, si), unroll=unroll)

    y = (jnp.dot(s_ref[...].astype(BF16), cbd_ref[...], preferred_element_type=F32)
         + d_ref[...] * u)
    if seg > 1:
        up_ref[...] = y

        def unpermute(k, _):
            y_ref[pl.ds(k, SUBLANES, stride=seg), :] = up_ref[rows8(k), :]
            return _
        lax.fori_loop(0, seg, unpermute, 0, unroll=unroll)
    else:
        y_ref[...] = y

    @pl.when(t == pl.num_programs(2) - 1)
    def _():
        hl_ref[...] = carry_ref[:1, :]


def _ssm_params(lam_re, lam_im, log_dt, b_re, b_im, c_re, c_im, d_skip):
    g, p = lam_re.shape
    c = b_re.shape[-1]
    nb = g // GROUPS_PER_BLOCK
    gb = GROUPS_PER_BLOCK
    lam = lax.complex(lam_re, lam_im)
    dt = jnp.exp(log_dt)[:, None]
    lam_bar = jnp.exp(lam * dt)
    b_bar = ((lam_bar - 1.0) / lam)[..., None] * lax.complex(b_re, b_im)
    eye = jnp.eye(gb, dtype=F32)

    def blockdiag_in(x):
        x = x.reshape(nb, gb, p, c).transpose(0, 1, 3, 2)
        x = x[:, :, :, None, :] * eye[None, :, None, :, None]
        return x.reshape(nb, gb * c, gb * p)

    bbd = jnp.concatenate([blockdiag_in(b_bar.real), blockdiag_in(b_bar.imag)], axis=-1)

    def blockdiag_out(x):
        x = x.reshape(nb, gb, c, p).transpose(0, 1, 3, 2)
        x = x[:, :, :, None, :] * eye[None, :, None, :, None]
        return x.reshape(nb, gb * p, gb * c)

    cbd = jnp.concatenate([blockdiag_out(c_re), -blockdiag_out(c_im)], axis=1)

    lam_blk = lam_bar.reshape(nb, 1, gb * p)
    lam_blk = jnp.concatenate([lam_blk.real, lam_blk.imag], axis=-1)
    dsk = d_skip.reshape(nb, 1, gb * c)
    return (bbd.astype(BF16), cbd.astype(BF16), dsk, lam_blk), lam_bar


def _segment_table(lam_bar, seg):
    g, p = lam_bar.shape
    nb = g // GROUPS_PER_BLOCK
    assert seg & (seg - 1) == 0
    base = lam_bar
    for _ in range(seg.bit_length() - 1):
        base = base * base
    pw = [base]
    for _ in range(SUBLANES - 1):
        pw.append(pw[-1] * base)
    pw = jnp.stack(pw)
    rows = jnp.arange(SUBLANES)[:, None, None]
    tbls = []
    for d in (1, 2, 4):
        m = jnp.where(rows >= d, jnp.broadcast_to(pw[d - 1], pw.shape), 0.0)
        tbls += [m.real, m.imag]
    tbls += [pw.real, pw.imag]
    tbl = jnp.stack(tbls)
    return tbl.reshape(8, SUBLANES, nb, GROUPS_PER_BLOCK * p).transpose(2, 0, 1, 3)


def _ssm(qu, h0, params, tbl, layer, *, tc):
    bbd, cbd, dsk, lam = params
    _, b, t, width = qu.shape
    _, nb, uw, sw = bbd.shape
    assert t % tc == 0 and tc % SUBLANES == 0
    seg = tc // SUBLANES
    bb = b if t == tc else 1
    lead = None if bb == 1 else bb
    carry = (SUBLANES, sw) if bb == 1 else (bb, SUBLANES, sw)
    return pl.pallas_call(
        _ssm_chunk if bb == 1 else _ssm_kernel,
        grid=(b // bb, nb, t // tc),
        in_specs=[
            pl.BlockSpec((None, lead, tc, uw), lambda bi, g, ti: (1, bi, ti, g)),
            pl.BlockSpec((lead, None, 1, sw), lambda bi, g, ti: (bi, g, 0, 0)),
            pl.BlockSpec((None, None, uw, sw), lambda bi, g, ti: (layer, g, 0, 0)),
            pl.BlockSpec((None, None, sw, uw), lambda bi, g, ti: (layer, g, 0, 0)),
            pl.BlockSpec((None, None, 1, uw), lambda bi, g, ti: (layer, g, 0, 0)),
            pl.BlockSpec((None, None, 1, sw), lambda bi, g, ti: (layer, g, 0, 0)),
            pl.BlockSpec((None, None, 8, SUBLANES, sw // 2),
                         lambda bi, g, ti: (layer, g, 0, 0, 0)),
        ],
        out_specs=[
            pl.BlockSpec((lead, tc, uw), lambda bi, g, ti: (bi, ti, g)),
            pl.BlockSpec((lead, None, 1, sw), lambda bi, g, ti: (bi, g, 0, 0)),
        ],
        out_shape=[jax.ShapeDtypeStruct((b, t, width), F32),
                   jax.ShapeDtypeStruct((b, nb, 1, sw), F32)],
        scratch_shapes=[pltpu.VMEM((SUBLANES * (seg + SUBLANES), uw), F32),
                        pltpu.VMEM((tc, uw), F32), pltpu.VMEM((tc, sw), F32),
                        pltpu.VMEM(carry, F32)],
        compiler_params=_params("parallel", "parallel", "arbitrary"),
        name="s5_scan",
    )(qu, h0, bbd, cbd, dsk, lam, tbl)


def _pack_state(re, im):
    b, g, p = re.shape
    nb = g // GROUPS_PER_BLOCK
    re = re.reshape(b, nb, 1, GROUPS_PER_BLOCK * p)
    im = im.reshape(b, nb, 1, GROUPS_PER_BLOCK * p)
    return jnp.concatenate([re, im], axis=-1)


def _unpack_state(hl, g, p):
    b = hl.shape[0]
    half = hl.shape[-1] // 2
    return hl[..., :half].reshape(b, g, p), hl[..., half:].reshape(b, g, p)


def _mix_kernel(h_ref, a_ref, y_ref, wglu_ref, gattn_ref, gssm_ref, wout_ref, gpost_ref, o_ref):
    y = y_ref[...]
    g = 0.5 * y * (1.0 + jnp.tanh(math.sqrt(2.0 / math.pi) * (y + 0.044715 * (y * y * y))))
    gate = jnp.dot(g.astype(BF16), wglu_ref[...], preferred_element_type=F32)
    ssm_out = g * jax.nn.sigmoid(gate)
    na = _rms(a_ref[...], gattn_ref[...]).astype(BF16)
    ns = _rms(ssm_out, gssm_ref[...]).astype(BF16)
    aw = na.shape[1]
    m = (jnp.dot(na, wout_ref[:aw, :], preferred_element_type=F32)
         + jnp.dot(ns, wout_ref[aw:, :], preferred_element_type=F32))
    o_ref[...] = h_ref[...] + _rms(m, gpost_ref[...])


def _mix(h, attn, y, w_glu, g_attn, g_ssm, w_out, g_post, layer, *, tm):
    m, d = h.shape
    aw = attn.shape[1]
    sw = y.shape[1]
    tm = min(tm, m)
    return pl.pallas_call(
        _mix_kernel,
        grid=(m // tm,),
        in_specs=[
            pl.BlockSpec((tm, d), lambda i: (i, 0)),
            pl.BlockSpec((tm, aw), lambda i: (i, 0)),
            pl.BlockSpec((tm, sw), lambda i: (i, 0)),
            pl.BlockSpec((None, sw, sw), lambda i: (layer, 0, 0)),
            pl.BlockSpec((None, 1, aw), lambda i: (layer, 0, 0)),
            pl.BlockSpec((None, 1, sw), lambda i: (layer, 0, 0)),
            pl.BlockSpec((None, d, d), lambda i: (layer, 0, 0)),
            pl.BlockSpec((None, 1, d), lambda i: (layer, 0, 0)),
        ],
        out_specs=pl.BlockSpec((tm, d), lambda i: (i, 0)),
        out_shape=jax.ShapeDtypeStruct((m, d), F32),
        compiler_params=_params("parallel"),
        name="mix_out",
    )(h, attn, y, w_glu, g_attn, g_ssm, w_out, g_post)


FFN_TM = 1024
FFN_TF = 512
PROJ_TM = 512
MIX_TM = 512
ATTN_TQ = 256
ATTN_HEADS_PER_STEP = 8
PAGES_PER_STEP = 16
SSM_TC = 1024


def kernel(x_prompt, x_sample, cache_k, cache_v, state_ssm_re, state_ssm_im, page_table, g_ffn1_pre, w_ffn1_gate, w_ffn1_up, w_ffn1_down, g_ffn1_post, g_mix_pre, w_in, sb_bias, lambda_re, lambda_im, log_dt, b_re, b_im, c_re, c_im, d_skip, w_glu, g_attn_out, g_ssm_out, w_out, g_mix_post, g_ffn2_pre, w_ffn2_gate, w_ffn2_up, w_ffn2_down, g_ffn2_post):
    depth = w_in.shape[0]
    bp, tp, d = x_prompt.shape
    bs, ts, _ = x_sample.shape
    groups, states = lambda_re.shape[1:]
    pool, page, heads = cache_k.shape[1:4]
    width = heads * HEAD_DIM
    ck = cache_k.reshape(depth, pool, page * heads, HEAD_DIM)
    cv = cache_v.reshape(depth, pool, page * heads, HEAD_DIM)

    hp = x_prompt.reshape(bp * tp, d)
    hs = x_sample.reshape(bs * ts, d)
    kp = jnp.zeros((depth, bp * tp, width), F32)
    vp = jnp.zeros((depth, bp * tp, width), F32)
    ks = jnp.zeros((depth, bs * ts, width), F32)
    vs = jnp.zeros((depth, bs * ts, width), F32)
    zero_state = jnp.zeros((bp, groups // GROUPS_PER_BLOCK, 1, 2 * GROUPS_PER_BLOCK * states), F32)
    states_out = [[] for _ in range(4)]
    tc_p = min(SSM_TC, tp)
    tc_s = min(SSM_TC, ts)

    gains = lambda g: g.reshape(depth, 1, -1)
    ffn1 = (gains(g_ffn1_pre), w_ffn1_gate, w_ffn1_up, w_ffn1_down, gains(g_ffn1_post))
    ffn2 = (gains(g_ffn2_pre), w_ffn2_gate, w_ffn2_up, w_ffn2_down, gains(g_ffn2_post))
    g_in = gains(g_mix_pre)
    w_in_b = w_in.astype(BF16)
    mixw = (w_glu.astype(BF16), gains(g_attn_out), gains(g_ssm_out), w_out.astype(BF16),
            gains(g_mix_post))
    s5, lam_bar = jax.vmap(_ssm_params)(lambda_re, lambda_im, log_dt, b_re, b_im, c_re, c_im,
                                        d_skip)
    tbl_p = jax.vmap(functools.partial(_segment_table, seg=tc_p // SUBLANES))(lam_bar)
    tbl_s = jax.vmap(functools.partial(_segment_table, seg=tc_s // SUBLANES))(lam_bar)

    for l in range(depth):

        hp, hs = _ffn(hp, hs, *ffn1, l, tm=FFN_TM, tf=FFN_TF)

        qup, kp, vp, qkv16 = _inproj(hp, g_in, w_in_b, kp, vp, l, tm=PROJ_TM)
        qus, ks, vs, _ = _inproj(hs, g_in, w_in_b, ks, vs, l, tm=PROJ_TM)
        qup = qup.reshape(2, bp, tp, width)
        qus = qus.reshape(2, bs, ts, width)

        attn_p = _attn_prompt(qkv16.reshape(3, bp, tp, width), sb_bias[l],
                              tq=ATTN_TQ, hps=ATTN_HEADS_PER_STEP)
        attn_s = _attn_sample(qus[0], ks[l].reshape(bs, ts, width), vs[l].reshape(bs, ts, width),
                              ck, cv, l, page_table, sb_bias[l], ppb=PAGES_PER_STEP)

        yp, hlp = _ssm(qup, zero_state, s5, tbl_p, l, tc=tc_p)
        ys, hls = _ssm(qus, _pack_state(state_ssm_re[l], state_ssm_im[l]), s5, tbl_s, l, tc=tc_s)

        hp = _mix(hp, attn_p.reshape(bp * tp, width), yp.reshape(bp * tp, -1), *mixw, l, tm=MIX_TM)
        hs = _mix(hs, attn_s.reshape(bs * ts, width), ys.reshape(bs * ts, -1), *mixw, l, tm=MIX_TM)

        hp, hs = _ffn(hp, hs, *ffn2, l, tm=FFN_TM, tf=FFN_TF)

        for dst, val in zip(states_out, _unpack_state(hlp, groups, states)
                            + _unpack_state(hls, groups, states)):
            dst.append(val)

    return (hp.reshape(bp, tp, d), hs.reshape(bs, ts, d),
            kp.reshape(depth, bp, tp, heads, HEAD_DIM), vp.reshape(depth, bp, tp, heads, HEAD_DIM),
            ks.reshape(depth, bs, ts, heads, HEAD_DIM), vs.reshape(depth, bs, ts, heads, HEAD_DIM),
            ) + tuple(jnp.stack(o) for o in states_out)
```

```python
import functools
import math

import jax
import jax.numpy as jnp
from jax import lax
from jax.experimental import pallas as pl
from jax.experimental.pallas import tpu as pltpu

F32 = jnp.float32
BF16 = jnp.bfloat16

RMS_EPS = 1e-6
HEAD_DIM = 128
MXU_DIM = 256
SSM_GROUP = 16
SSM_STATE = 64
GROUPS_PER_BLOCK = 8
SUBLANES = 8
VMEM_LIMIT = 62 * 1024 * 1024


def _params(*sem):
    return pltpu.CompilerParams(dimension_semantics=sem, vmem_limit_bytes=VMEM_LIMIT)


def _rms(x, g):
    return x * lax.rsqrt(jnp.mean(x * x, axis=-1, keepdims=True) + RMS_EPS) * g


LOG2E = math.log2(math.e)


EXP2_GUARD = 100.0


def _softplus2(z2):
    return jnp.where(z2 > EXP2_GUARD, z2, jnp.log2(1.0 + jnp.exp2(z2)))


def _nt_dot(a, b):
    return lax.dot_general(a, b, (((1,), (1,)), ((), ())), preferred_element_type=F32)


def _suffix_matrix(s):
    r = lax.broadcasted_iota(jnp.int32, (2 * s, s), 0)
    c = lax.broadcasted_iota(jnp.int32, (2 * s, s), 1)
    return jnp.where((r > c) & ((r < s) | (r > c + s)), 1.0, 0.0).astype(BF16)


def _sb_split(z, s, causal=None):
    m, w = z.shape
    n = w // s
    sp = _softplus2(z)
    log_beta = z - sp
    if causal is not None:
        sp = jnp.where(causal, sp, 0.0)
    if n > 1:
        sp = jnp.concatenate([sp[:, i * s:(i + 1) * s] for i in range(n)], axis=0)
    hi = sp.astype(BF16)
    lo = (sp - hi.astype(F32)).astype(BF16)
    return log_beta, sp[:, :1], jnp.concatenate([hi, lo], axis=1)


def _sb_finish(log_beta, sp0, after, carry, causal=None):
    m, w = log_beta.shape
    n = after.shape[0] // m
    total = after[:, :1] + sp0
    parts = [None] * n
    for i in reversed(range(n)):
        parts[i] = after[i * m:(i + 1) * m] + carry
        carry = carry + total[i * m:(i + 1) * m]
    after = parts[0] if n == 1 else jnp.concatenate(parts, axis=1)
    wgt = jnp.exp2(log_beta - after)
    if causal is not None:
        wgt = jnp.where(causal, wgt, 0.0)
    return wgt.astype(BF16), carry


def _ffn_kernel(x_ref, xs_ref, gpre_ref, wg_ref, wu_ref, wd_ref, gpost_ref, o_ref, os_ref, n_ref):
    i = pl.program_id(0)
    f = pl.program_id(1)
    last = pl.num_programs(1) - 1
    tm = x_ref.shape[0]

    @pl.when(f == 0)
    def _():
        n_ref[:tm, :] = _rms(x_ref[...], gpre_ref[...]).astype(BF16)
        o_ref[...] = jnp.zeros_like(o_ref)

    @pl.when((f == 0) & (i == 0))
    def _():
        n_ref[tm:, :] = _rms(xs_ref[...], gpre_ref[...]).astype(BF16)
        os_ref[...] = jnp.zeros_like(os_ref)

    def swiglu(n):
        gate = jnp.dot(n, wg_ref[...].astype(BF16), preferred_element_type=F32)
        up = jnp.dot(n, wu_ref[...].astype(BF16), preferred_element_type=F32)
        act = gate * jax.nn.sigmoid(gate) * up
        return jnp.dot(act.astype(BF16), wd_ref[...].astype(BF16), preferred_element_type=F32)

    @pl.when(i == 0)
    def _():
        res = swiglu(n_ref[...])
        o_ref[...] += res[:tm]
        os_ref[...] += res[tm:]

    @pl.when(i > 0)
    def _():
        o_ref[...] += swiglu(n_ref[:tm, :])

    @pl.when(f == last)
    def _():
        o_ref[...] = x_ref[...] + 0.5 * _rms(o_ref[...], gpost_ref[...])

    @pl.when((f == last) & (i == 0))
    def _():
        os_ref[...] = xs_ref[...] + 0.5 * _rms(os_ref[...], gpost_ref[...])


def _ffn(x, xs, g_pre, w_gate, w_up, w_down, g_post, layer, *, tm, tf):
    m, d = x.shape
    ms = xs.shape[0]
    dff = w_gate.shape[2]
    tm = min(tm, m)
    return pl.pallas_call(
        _ffn_kernel,
        grid=(m // tm, dff // tf),
        in_specs=[
            pl.BlockSpec((tm, d), lambda i, f: (i, 0)),
            pl.BlockSpec((ms, d), lambda i, f: (0, 0)),
            pl.BlockSpec((None, 1, d), lambda i, f: (layer, 0, 0)),
            pl.BlockSpec((None, d, tf), lambda i, f: (layer, 0, f)),
            pl.BlockSpec((None, d, tf), lambda i, f: (layer, 0, f)),
            pl.BlockSpec((None, tf, d), lambda i, f: (layer, f, 0)),
            pl.BlockSpec((None, 1, d), lambda i, f: (layer, 0, 0)),
        ],
        out_specs=[pl.BlockSpec((tm, d), lambda i, f: (i, 0), pipeline_mode=pl.Buffered(1)),
                   pl.BlockSpec((ms, d), lambda i, f: (0, 0))],
        out_shape=[jax.ShapeDtypeStruct((m, d), F32), jax.ShapeDtypeStruct((ms, d), F32)],
        scratch_shapes=[pltpu.VMEM((tm + ms, d), BF16)],
        compiler_params=_params("arbitrary", "arbitrary"),
        name="ffn_half",
    )(x, xs, g_pre, w_gate, w_up, w_down, g_post)


def _inproj_kernel(x_ref, g_ref, w_ref, kin_ref, vin_ref, qu_ref, k_ref, v_ref, qkv16_ref, n_ref,
                   *, q_scale):
    del kin_ref, vin_ref
    j = pl.program_id(1)

    @pl.when(j == 0)
    def _():
        n_ref[...] = _rms(x_ref[...], g_ref[...]).astype(BF16)

    project = lambda: jnp.dot(n_ref[...], w_ref[...], preferred_element_type=F32)

    @pl.when(j == 0)
    def _():
        res = project()
        qu_ref[...] = res
        qkv16_ref[...] = (res * q_scale).astype(BF16)

    @pl.when(j == 1)
    def _():
        res = project()
        k_ref[...] = res
        qkv16_ref[...] = res.astype(BF16)

    @pl.when(j == 2)
    def _():
        res = project()
        v_ref[...] = res
        qkv16_ref[...] = res.astype(BF16)

    @pl.when(j == 3)
    def _():
        qu_ref[...] = project()


def _inproj(x, g, w_in, k_all, v_all, layer, *, tm):
    m, d = x.shape
    width = w_in.shape[2] // 4
    tm = min(tm, m)
    return pl.pallas_call(
        functools.partial(_inproj_kernel, q_scale=HEAD_DIM ** -0.5 * LOG2E),
        grid=(m // tm, 4),
        in_specs=[
            pl.BlockSpec((tm, d), lambda i, j: (i, 0)),
            pl.BlockSpec((None, 1, d), lambda i, j: (layer, 0, 0)),
            pl.BlockSpec((None, d, width), lambda i, j: (layer, 0, j)),
            pl.BlockSpec(memory_space=pl.ANY),
            pl.BlockSpec(memory_space=pl.ANY),
        ],
        out_specs=[
            pl.BlockSpec((None, tm, width), lambda i, j: (j // 3, i, 0)),
            pl.BlockSpec((None, tm, width), lambda i, j: (layer, i, 0)),
            pl.BlockSpec((None, tm, width), lambda i, j: (layer, i, 0)),
            pl.BlockSpec((None, tm, width), lambda i, j: (jnp.minimum(j, 2), i, 0)),
        ],
        out_shape=[jax.ShapeDtypeStruct((2, m, width), F32),
                   jax.ShapeDtypeStruct(k_all.shape, F32),
                   jax.ShapeDtypeStruct(v_all.shape, F32),
                   jax.ShapeDtypeStruct((3, m, width), BF16)],
        input_output_aliases={3: 1, 4: 2},
        scratch_shapes=[pltpu.VMEM((tm, d), BF16)],
        compiler_params=_params("parallel", "arbitrary"),
        name="in_proj",
    )(x, g, w_in, k_all, v_all)


def _attn_kernel(bias_ref, q_ref, k_ref, v_ref, o_ref, *, tq, hps):
    hb = pl.program_id(1)
    i = pl.program_id(2)
    sub = min(tq, MXU_DIM)
    suffix = _suffix_matrix(sub)
    row = lax.broadcasted_iota(jnp.int32, (tq, tq), 0)
    col = lax.broadcasted_iota(jnp.int32, (tq, tq), 1)
    lanes = [slice(n * HEAD_DIM, (n + 1) * HEAD_DIM) for n in range(hps)]
    biases = [bias_ref[hb * hps + n] * LOG2E for n in range(hps)]

    def step(j, nblk, state, masked):
        keys = pl.ds(pl.multiple_of(j * tq, nblk * tq), nblk * tq)
        causal = (col < row) if masked else None
        zs = [_nt_dot(q_ref[:, ln], k_ref[keys, ln]) + biases[n] for n, ln in enumerate(lanes)]
        split = [_sb_split(z, sub, causal) for z in zs]
        afters = [jnp.dot(st, suffix, preferred_element_type=F32) for _, _, st in split]
        new = []
        for n, ln in enumerate(lanes):
            carry, acc = state[n]
            w, carry = _sb_finish(split[n][0], split[n][1], afters[n], carry, causal)
            new.append((carry, acc + jnp.dot(w, v_ref[keys, ln], preferred_element_type=F32)))
        return tuple(new)

    state = tuple((jnp.zeros((tq, 1), F32), jnp.zeros((tq, HEAD_DIM), F32)) for _ in lanes)
    state = step(i, 1, state, True)
    state = lax.fori_loop(0, i % 2, lambda t, s: step(i - 1, 1, s, False), state)
    pairs = i // 2
    state = lax.fori_loop(0, pairs, lambda t, s: step(2 * (pairs - 1 - t), 2, s, False), state)
    for n, ln in enumerate(lanes):
        o_ref[:, ln] = state[n][1]


def _attn_prompt(qkv16, sb_bias, *, tq, hps):
    _, b, t, width = qkv16.shape
    heads = width // HEAD_DIM
    tq = min(tq, t)
    hps = min(hps, heads)
    wide = hps * HEAD_DIM
    return pl.pallas_call(
        functools.partial(_attn_kernel, tq=tq, hps=hps),
        grid=(b, heads // hps, t // tq),
        in_specs=[
            pl.BlockSpec(memory_space=pltpu.SMEM),
            pl.BlockSpec((None, None, tq, wide), lambda bi, h, i: (0, bi, i, h)),
            pl.BlockSpec((None, None, t, wide), lambda bi, h, i: (1, bi, 0, h)),
            pl.BlockSpec((None, None, t, wide), lambda bi, h, i: (2, bi, 0, h)),
        ],
        out_specs=pl.BlockSpec((None, tq, wide), lambda bi, h, i: (bi, i, h)),
        out_shape=jax.ShapeDtypeStruct((b, t, width), F32),
        compiler_params=_params("parallel", "parallel", "arbitrary"),
        name="attn_prompt",
    )(sb_bias, qkv16, qkv16, qkv16)


def _sattn_kernel(pt_ref, qbd_ref, bias_ref, kn_ref, vn_ref, *rest,
                  scale, heads, tsteps, ppb):
    del pt_ref
    kc_refs, vc_refs = rest[:ppb], rest[ppb:2 * ppb]
    o_ref, acc_ref, carry_ref = rest[2 * ppb:]
    j = pl.program_id(1)
    rows = carry_ref.shape[0]
    page = kn_ref.shape[0]
    qbd = qbd_ref[...]

    def process(kb, vb, bias, causal):
        z = _nt_dot(qbd, kb) * (scale * LOG2E) + bias
        sub = min(z.shape[1], MXU_DIM)
        log_beta, sp0, stack = _sb_split(z, sub, causal)
        after = jnp.dot(stack, _suffix_matrix(sub), preferred_element_type=F32)
        w, carry = _sb_finish(log_beta, sp0, after, carry_ref[:, :1], causal)
        acc_ref[...] += jnp.dot(w, vb, preferred_element_type=F32)
        carry_ref[...] = jnp.broadcast_to(carry, carry_ref.shape)

    @pl.when(j == 0)
    def _():
        acc_ref[...] = jnp.zeros_like(acc_ref)
        carry_ref[...] = jnp.zeros_like(carry_ref)
        row = lax.broadcasted_iota(jnp.int32, (rows, page), 0)
        col = lax.broadcasted_iota(jnp.int32, (rows, page), 1)
        process(kn_ref[...].astype(BF16), vn_ref[...].astype(BF16), bias_ref[:, :page],
                col < (row % tsteps))

    def gather(ref):
        return jnp.concatenate(
            [ref[pl.ds(h, page, stride=heads), :].astype(BF16) for h in range(heads)], axis=1)

    kb = jnp.concatenate([gather(r) for r in kc_refs], axis=0)
    vb = jnp.concatenate([gather(r) for r in vc_refs], axis=0)
    process(kb, vb, bias_ref[...], None)

    @pl.when(j == pl.num_programs(1) - 1)
    def _():
        for h in range(heads):
            o_ref[:, h * HEAD_DIM:(h + 1) * HEAD_DIM] = (
                acc_ref[h * tsteps:(h + 1) * tsteps, h * HEAD_DIM:(h + 1) * HEAD_DIM])


def _attn_sample(q, k_new, v_new, cache_k, cache_v, layer, page_table, sb_bias, *, ppb):
    b, ts, width = q.shape
    heads = width // HEAD_DIM
    page = cache_k.shape[2] // heads
    n_pages = page_table.shape[1]
    ppb = min(ppb, n_pages)
    rows = heads * ts
    q = q.reshape(b, ts, heads, HEAD_DIM)
    eye = jnp.eye(heads, dtype=F32)
    qbd = (q.transpose(0, 2, 1, 3)[:, :, :, None, :] * eye[None, :, None, :, None])
    qbd = qbd.reshape(b, rows, width).astype(BF16)
    bias = jnp.broadcast_to(jnp.repeat(sb_bias * LOG2E, ts)[:, None], (rows, ppb * page))
    pad = ((0, 0), (0, page - ts), (0, 0))
    k_new = jnp.pad(k_new, pad)
    v_new = jnp.pad(v_new, pad)
    kern = functools.partial(_sattn_kernel, scale=HEAD_DIM ** -0.5, heads=heads, tsteps=ts,
                             ppb=ppb)

    def page_spec(r):
        return pl.BlockSpec(
            (None, None, page * heads, HEAD_DIM),
            lambda bi, j, pt: (layer, pt[bi, n_pages - ppb * (j + 1) + r], 0, 0))

    return pl.pallas_call(
        kern,
        grid_spec=pltpu.PrefetchScalarGridSpec(
            num_scalar_prefetch=1,
            grid=(b, n_pages // ppb),
            in_specs=[
                pl.BlockSpec((None, rows, width), lambda bi, j, pt: (bi, 0, 0)),
                pl.BlockSpec((rows, ppb * page), lambda bi, j, pt: (0, 0)),
                pl.BlockSpec((None, page, width), lambda bi, j, pt: (bi, 0, 0)),
                pl.BlockSpec((None, page, width), lambda bi, j, pt: (bi, 0, 0)),
            ] + [page_spec(r) for r in range(ppb)] * 2,
            out_specs=pl.BlockSpec((None, ts, width), lambda bi, j, pt: (bi, 0, 0)),
            scratch_shapes=[pltpu.VMEM((rows, width), F32), pltpu.VMEM((rows, HEAD_DIM), F32)],
        ),
        out_shape=jax.ShapeDtypeStruct((b, ts, width), F32),
        compiler_params=_params("parallel", "arbitrary"),
        name="attn_sample",
    )(page_table, qbd, bias, k_new, v_new, *([cache_k] * ppb), *([cache_v] * ppb))


def _cmul_add(xr, xi, ar, ai, br, bi):
    return xr + (ar * br - ai * bi), xi + (ar * bi + ai * br)


def _ssm_kernel(u_ref, h0_ref, bbd_ref, cbd_ref, d_ref, lam_ref, tbl_ref, y_ref, hl_ref,
                pad_ref, up_ref, s_ref, carry_ref):
    for b in range(u_ref.shape[0]):
        _ssm_chunk(u_ref.at[b], h0_ref.at[b], bbd_ref, cbd_ref, d_ref, lam_ref, tbl_ref,
                   y_ref.at[b], hl_ref.at[b], pad_ref, up_ref, s_ref, carry_ref.at[b])


def _ssm_chunk(u_ref, h0_ref, bbd_ref, cbd_ref, d_ref, lam_ref, tbl_ref, y_ref, hl_ref,
               pad_ref, up_ref, s_ref, carry_ref):
    t = pl.program_id(2)
    tc, lanes = s_ref.shape
    half = lanes // 2
    seg = tc // SUBLANES
    unroll = min(seg, 8)
    rows8 = lambda k: pl.ds(pl.multiple_of(k * SUBLANES, SUBLANES), SUBLANES)

    @pl.when(t == 0)
    def _():
        carry_ref[...] = jnp.broadcast_to(h0_ref[...], carry_ref.shape)

    if seg > 1:
        pitch = seg + SUBLANES
        for j in range(SUBLANES):
            pad_ref[j * pitch:j * pitch + seg, :] = u_ref[j * seg:(j + 1) * seg, :]

        def permute(k, _):
            up_ref[rows8(k), :] = pad_ref[pl.ds(k, SUBLANES, stride=pitch), :]
            return _
        lax.fori_loop(0, seg, permute, 0, unroll=unroll)
        u = up_ref[...]
    else:
        u = u_ref[...]
    s_ref[...] = jnp.dot(u.astype(BF16), bbd_ref[...], preferred_element_type=F32)

    lam_r = jnp.broadcast_to(lam_ref[:, :half], (SUBLANES, half))
    lam_i = jnp.broadcast_to(lam_ref[:, half:], (SUBLANES, half))

    def local(k, c):
        return _cmul_add(s_ref[rows8(k), :half], s_ref[rows8(k), half:], lam_r, lam_i, *c)

    zero = jnp.zeros((SUBLANES, half), F32)
    xr, xi = lax.fori_loop(0, seg, local, (zero, zero), unroll=unroll)

    for n, d in enumerate((1, 2, 4)):
        xr, xi = _cmul_add(xr, xi, tbl_ref[2 * n], tbl_ref[2 * n + 1],
                           pltpu.roll(xr, d, 0), pltpu.roll(xi, d, 0))
    cr = carry_ref[:, :half]
    ci = carry_ref[:, half:]
    fr, fi = _cmul_add(xr, xi, tbl_ref[6], tbl_ref[7], cr, ci)
    first = lax.broadcasted_iota(jnp.int32, (SUBLANES, half), 0) == 0
    sr = jnp.where(first, cr, pltpu.roll(fr, 1, 0))
    si = jnp.where(first, ci, pltpu.roll(fi, 1, 0))
    last = SUBLANES - 1
    carry_ref[:, :half] = jnp.broadcast_to(fr[last:, :], (SUBLANES, half))
    carry_ref[:, half:] = jnp.broadcast_to(fi[last:, :], (SUBLANES, half))

    def true_scan(k, c):
        hr, hi = _cmul_add(s_ref[rows8(k), :half], s_ref[rows8(k), half:], lam_r, lam_i, *c)
        s_ref[rows8(k), :half] = hr
        s_ref[rows8(k), half:] = hi
        return hr, hi

    lax.fori_loop(0, seg, true_scan, (sr, si), unroll=unroll)

    y = (jnp.dot(s_ref[...].astype(BF16), cbd_ref[...], preferred_element_type=F32)
         + d_ref[...] * u)
    if seg > 1:
        up_ref[...] = y

        def unpermute(k, _):
            y_ref[pl.ds(k, SUBLANES, stride=seg), :] = up_ref[rows8(k), :]
            return _
        lax.fori_loop(0, seg, unpermute, 0, unroll=unroll)
    else:
        y_ref[...] = y

    @pl.when(t == pl.num_programs(2) - 1)
    def _():
        hl_ref[...] = carry_ref[:1, :]


def _ssm_params(lam_re, lam_im, log_dt, b_re, b_im, c_re, c_im, d_skip):
    g, p = lam_re.shape
    c = b_re.shape[-1]
    nb = g // GROUPS_PER_BLOCK
    gb = GROUPS_PER_BLOCK
    lam = lax.complex(lam_re, lam_im)
    dt = jnp.exp(log_dt)[:, None]
    lam_bar = jnp.exp(lam * dt)
    b_bar = ((lam_bar - 1.0) / lam)[..., None] * lax.complex(b_re, b_im)
    eye = jnp.eye(gb, dtype=F32)

    def blockdiag_in(x):
        x = x.reshape(nb, gb, p, c).transpose(0, 1, 3, 2)
        x = x[:, :, :, None, :] * eye[None, :, None, :, None]
        return x.reshape(nb, gb * c, gb * p)

    bbd = jnp.concatenate([blockdiag_in(b_bar.real), blockdiag_in(b_bar.imag)], axis=-1)

    def blockdiag_out(x):
        x = x.reshape(nb, gb, c, p).transpose(0, 1, 3, 2)
        x = x[:, :, :, None, :] * eye[None, :, None, :, None]
        return x.reshape(nb, gb * p, gb * c)

    cbd = jnp.concatenate([blockdiag_out(c_re), -blockdiag_out(c_im)], axis=1)

    lam_blk = lam_bar.reshape(nb, 1, gb * p)
    lam_blk = jnp.concatenate([lam_blk.real, lam_blk.imag], axis=-1)
    dsk = d_skip.reshape(nb, 1, gb * c)
    return (bbd.astype(BF16), cbd.astype(BF16), dsk, lam_blk), lam_bar


def _segment_table(lam_bar, seg):
    g, p = lam_bar.shape
    nb = g // GROUPS_PER_BLOCK
    assert seg & (seg - 1) == 0
    base = lam_bar
    for _ in range(seg.bit_length() - 1):
        base = base * base
    pw = [base]
    for _ in range(SUBLANES - 1):
        pw.append(pw[-1] * base)
    pw = jnp.stack(pw)
    rows = jnp.arange(SUBLANES)[:, None, None]
    tbls = []
    for d in (1, 2, 4):
        m = jnp.where(rows >= d, jnp.broadcast_to(pw[d - 1], pw.shape), 0.0)
        tbls += [m.real, m.imag]
    tbls += [pw.real, pw.imag]
    tbl = jnp.stack(tbls)
    return tbl.reshape(8, SUBLANES, nb, GROUPS_PER_BLOCK * p).transpose(2, 0, 1, 3)


def _ssm(qu, h0, params, tbl, layer, *, tc):
    bbd, cbd, dsk, lam = params
    _, b, t, width = qu.shape
    _, nb, uw, sw = bbd.shape
    assert t % tc == 0 and tc % SUBLANES == 0
    seg = tc // SUBLANES
    bb = b if t == tc else 1
    lead = None if bb == 1 else bb
    carry = (SUBLANES, sw) if bb == 1 else (bb, SUBLANES, sw)
    return pl.pallas_call(
        _ssm_chunk if bb == 1 else _ssm_kernel,
        grid=(b // bb, nb, t // tc),
        in_specs=[
            pl.BlockSpec((None, lead, tc, uw), lambda bi, g, ti: (1, bi, ti, g)),
            pl.BlockSpec((lead, None, 1, sw), lambda bi, g, ti: (bi, g, 0, 0)),
            pl.BlockSpec((None, None, uw, sw), lambda bi, g, ti: (layer, g, 0, 0)),
            pl.BlockSpec((None, None, sw, uw), lambda bi, g, ti: (layer, g, 0, 0)),
            pl.BlockSpec((None, None, 1, uw), lambda bi, g, ti: (layer, g, 0, 0)),
            pl.BlockSpec((None, None, 1, sw), lambda bi, g, ti: (layer, g, 0, 0)),
            pl.BlockSpec((None, None, 8, SUBLANES, sw // 2),
                         lambda bi, g, ti: (layer, g, 0, 0, 0)),
        ],
        out_specs=[
            pl.BlockSpec((lead, tc, uw), lambda bi, g, ti: (bi, ti, g)),
            pl.BlockSpec((lead, None, 1, sw), lambda bi, g, ti: (bi, g, 0, 0)),
        ],
        out_shape=[jax.ShapeDtypeStruct((b, t, width), F32),
                   jax.ShapeDtypeStruct((b, nb, 1, sw), F32)],
        scratch_shapes=[pltpu.VMEM((SUBLANES * (seg + SUBLANES), uw), F32),
                        pltpu.VMEM((tc, uw), F32), pltpu.VMEM((tc, sw), F32),
                        pltpu.VMEM(carry, F32)],
        compiler_params=_params("parallel", "parallel", "arbitrary"),
        name="s5_scan",
    )(qu, h0, bbd, cbd, dsk, lam, tbl)


def _pack_state(re, im):
    b, g, p = re.shape
    nb = g // GROUPS_PER_BLOCK
    re = re.reshape(b, nb, 1, GROUPS_PER_BLOCK * p)
    im = im.reshape(b, nb, 1, GROUPS_PER_BLOCK * p)
    return jnp.concatenate([re, im], axis=-1)


def _unpack_state(hl, g, p):
    b = hl.shape[0]
    half = hl.shape[-1] // 2
    return hl[..., :half].reshape(b, g, p), hl[..., half:].reshape(b, g, p)


def _mix_kernel(h_ref, a_ref, y_ref, wglu_ref, gattn_ref, gssm_ref, wout_ref, gpost_ref, o_ref):
    y = y_ref[...]
    g = 0.5 * y * (1.0 + jnp.tanh(math.sqrt(2.0 / math.pi) * (y + 0.044715 * (y * y * y))))
    gate = jnp.dot(g.astype(BF16), wglu_ref[...], preferred_element_type=F32)
    ssm_out = g * jax.nn.sigmoid(gate)
    na = _rms(a_ref[...], gattn_ref[...]).astype(BF16)
    ns = _rms(ssm_out, gssm_ref[...]).astype(BF16)
    aw = na.shape[1]
    m = (jnp.dot(na, wout_ref[:aw, :], preferred_element_type=F32)
         + jnp.dot(ns, wout_ref[aw:, :], preferred_element_type=F32))
    o_ref[...] = h_ref[...] + _rms(m, gpost_ref[...])


def _mix(h, attn, y, w_glu, g_attn, g_ssm, w_out, g_post, layer, *, tm):
    m, d = h.shape
    aw = attn.shape[1]
    sw = y.shape[1]
    tm = min(tm, m)
    return pl.pallas_call(
        _mix_kernel,
        grid=(m // tm,),
        in_specs=[
            pl.BlockSpec((tm, d), lambda i: (i, 0)),
            pl.BlockSpec((tm, aw), lambda i: (i, 0)),
            pl.BlockSpec((tm, sw), lambda i: (i, 0)),
            pl.BlockSpec((None, sw, sw), lambda i: (layer, 0, 0)),
            pl.BlockSpec((None, 1, aw), lambda i: (layer, 0, 0)),
            pl.BlockSpec((None, 1, sw), lambda i: (layer, 0, 0)),
            pl.BlockSpec((None, d, d), lambda i: (layer, 0, 0)),
            pl.BlockSpec((None, 1, d), lambda i: (layer, 0, 0)),
        ],
        out_specs=pl.BlockSpec((tm, d), lambda i: (i, 0)),
        out_shape=jax.ShapeDtypeStruct((m, d), F32),
        compiler_params=_params("parallel"),
        name="mix_out",
    )(h, attn, y, w_glu, g_attn, g_ssm, w_out, g_post)


FFN_TM = 1024
FFN_TF = 512
PROJ_TM = 512
MIX_TM = 512
ATTN_TQ = 256
ATTN_HEADS_PER_STEP = 8
PAGES_PER_STEP = 16
SSM_TC = 2048


def kernel(x_prompt, x_sample, cache_k, cache_v, state_ssm_re, state_ssm_im, page_table, g_ffn1_pre, w_ffn1_gate, w_ffn1_up, w_ffn1_down, g_ffn1_post, g_mix_pre, w_in, sb_bias, lambda_re, lambda_im, log_dt, b_re, b_im, c_re, c_im, d_skip, w_glu, g_attn_out, g_ssm_out, w_out, g_mix_post, g_ffn2_pre, w_ffn2_gate, w_ffn2_up, w_ffn2_down, g_ffn2_post):
    depth = w_in.shape[0]
    bp, tp, d = x_prompt.shape
    bs, ts, _ = x_sample.shape
    groups, states = lambda_re.shape[1:]
    pool, page, heads = cache_k.shape[1:4]
    width = heads * HEAD_DIM
    ck = cache_k.reshape(depth, pool, page * heads, HEAD_DIM)
    cv = cache_v.reshape(depth, pool, page * heads, HEAD_DIM)

    hp = x_prompt.reshape(bp * tp, d)
    hs = x_sample.reshape(bs * ts, d)
    kp = jnp.zeros((depth, bp * tp, width), F32)
    vp = jnp.zeros((depth, bp * tp, width), F32)
    ks = jnp.zeros((depth, bs * ts, width), F32)
    vs = jnp.zeros((depth, bs * ts, width), F32)
    zero_state = jnp.zeros((bp, groups // GROUPS_PER_BLOCK, 1, 2 * GROUPS_PER_BLOCK * states), F32)
    states_out = [[] for _ in range(4)]
    tc_p = min(SSM_TC, tp)
    tc_s = min(SSM_TC, ts)

    gains = lambda g: g.reshape(depth, 1, -1)
    ffn1 = (gains(g_ffn1_pre), w_ffn1_gate, w_ffn1_up, w_ffn1_down, gains(g_ffn1_post))
    ffn2 = (gains(g_ffn2_pre), w_ffn2_gate, w_ffn2_up, w_ffn2_down, gains(g_ffn2_post))
    g_in = gains(g_mix_pre)
    w_in_b = w_in.astype(BF16)
    mixw = (w_glu.astype(BF16), gains(g_attn_out), gains(g_ssm_out), w_out.astype(BF16),
            gains(g_mix_post))
    s5, lam_bar = jax.vmap(_ssm_params)(lambda_re, lambda_im, log_dt, b_re, b_im, c_re, c_im,
                                        d_skip)
    tbl_p = jax.vmap(functools.partial(_segment_table, seg=tc_p // SUBLANES))(lam_bar)
    tbl_s = jax.vmap(functools.partial(_segment_table, seg=tc_s // SUBLANES))(lam_bar)

    for l in range(depth):

        hp, hs = _ffn(hp, hs, *ffn1, l, tm=FFN_TM, tf=FFN_TF)

        qup, kp, vp, qkv16 = _inproj(hp, g_in, w_in_b, kp, vp, l, tm=PROJ_TM)
        qus, ks, vs, _ = _inproj(hs, g_in, w_in_b, ks, vs, l, tm=PROJ_TM)
        qup = qup.reshape(2, bp, tp, width)
        qus = qus.reshape(2, bs, ts, width)

        attn_p = _attn_prompt(qkv16.reshape(3, bp, tp, width), sb_bias[l],
                              tq=ATTN_TQ, hps=ATTN_HEADS_PER_STEP)
        attn_s = _attn_sample(qus[0], ks[l].reshape(bs, ts, width), vs[l].reshape(bs, ts, width),
                              ck, cv, l, page_table, sb_bias[l], ppb=PAGES_PER_STEP)

        yp, hlp = _ssm(qup, zero_state, s5, tbl_p, l, tc=tc_p)
        ys, hls = _ssm(qus, _pack_state(state_ssm_re[l], state_ssm_im[l]), s5, tbl_s, l, tc=tc_s)

        hp = _mix(hp, attn_p.reshape(bp * tp, width), yp.reshape(bp * tp, -1), *mixw, l, tm=MIX_TM)
        hs = _mix(hs, attn_s.reshape(bs * ts, width), ys.reshape(bs * ts, -1), *mixw, l, tm=MIX_TM)

        hp, hs = _ffn(hp, hs, *ffn2, l, tm=FFN_TM, tf=FFN_TF)

        for dst, val in zip(states_out, _unpack_state(hlp, groups, states)
                            + _unpack_state(hls, groups, states)):
            dst.append(val)

    return (hp.reshape(bp, tp, d), hs.reshape(bs, ts, d),
            kp.reshape(depth, bp, tp, heads, HEAD_DIM), vp.reshape(depth, bp, tp, heads, HEAD_DIM),
            ks.reshape(depth, bs, ts, heads, HEAD_DIM), vs.reshape(depth, bs, ts, heads, HEAD_DIM),
            ) + tuple(jnp.stack(o) for o in states_out)
```
